```python
import jax, jax.numpy as jnp
from jax import lax
import numpy as np

D_MODEL = 1024
BATCH = 2
SEQ = 16384
DEPTH = 1
DEC_BATCH = 32
DEC_SEQ = 64
PAST_LEN = 1024

CHUNK = 64
POOL_W = 512
POOL_WINDOWS = (2, 4, 8, 16)
POOL_GROUPS = len(POOL_WINDOWS)
POOL_GW = POOL_W // POOL_GROUPS
POOL_HIST = max(POOL_WINDOWS) - 1
CONV_W = 512
CONV_K = 31
CONV_HIST = CONV_K - 1
MIX_W = POOL_W + CONV_W
IN_W = POOL_W + 2 * CONV_W
PEER_HEADS = 8
PEER_NK = 128
PEER_N = PEER_NK * PEER_NK
PEER_DK = 256
PEER_DKH = PEER_DK // 2
PEER_TOPK = 16
PEER_BLOCK = 128
EPS = 1e-6

kernel_name = "hybrid_pool_conformer_peer_stream_step"


def rmsnorm(x, g):
    xf = x.astype(jnp.float32)
    y = xf * lax.rsqrt(jnp.mean(xf * xf, axis=-1, keepdims=True) + EPS)
    return (y * g.astype(jnp.float32)).astype(x.dtype)


def layernorm(x, g, b):
    xf = x.astype(jnp.float32)
    mu = jnp.mean(xf, axis=-1, keepdims=True)
    var = jnp.mean(jnp.square(xf - mu), axis=-1, keepdims=True)
    y = (xf - mu) * lax.rsqrt(var + EPS)
    return (y * g.astype(jnp.float32) + b.astype(jnp.float32)).astype(x.dtype)


def pool_mixer(za, hist, pos0, w_pool, b_pool, pool_scale):
    T = za.shape[1]
    hp = jnp.concatenate([hist, za], axis=1)
    cs = jnp.cumsum(hp.astype(jnp.float32), axis=1)
    cs = jnp.pad(cs, ((0, 0), (1, 0), (0, 0)))
    pos = pos0 + jnp.arange(T)
    outs = []
    for g, w in enumerate(POOL_WINDOWS):
        sl = slice(g * POOL_GW, (g + 1) * POOL_GW)
        hi = POOL_HIST + 1
        lo = POOL_HIST + 1 - w
        tot = cs[:, hi:hi + T, sl] - cs[:, lo:lo + T, sl]
        cnt = jnp.minimum(pos + 1, w).astype(jnp.float32)[None, :, None]
        d = (tot / cnt).astype(za.dtype) - za[..., sl]
        outs.append(d @ w_pool[g] + b_pool[g])
    return jnp.concatenate(outs, axis=-1) * pool_scale


def causal_depthwise_conv(up, w_dw):
    return lax.conv_general_dilated(
        up, w_dw[:, None, :], window_strides=(1,), padding='VALID',
        dimension_numbers=('NWC', 'WIO', 'NWC'), feature_group_count=up.shape[-1])


def peer(x, w_query, sub_keys, peer_u, peer_v):
    B, T, D = x.shape
    xf = x.reshape(B * T, D)
    n = B * T
    pad = (-n) % PEER_BLOCK
    xf = jnp.pad(xf, ((0, pad), (0, 0)))
    xb = xf.reshape(-1, PEER_BLOCK, D)

    def block(xt):
        q = (xt @ w_query).reshape(PEER_BLOCK, PEER_HEADS, 2, PEER_DKH)
        s = jnp.einsum('thpk,hpnk->thpn', q, sub_keys).astype(jnp.float32)
        s_top, i_top = lax.top_k(s, PEER_TOPK)
        comb = (s_top[:, :, 0, :, None] + s_top[:, :, 1, None, :]).reshape(PEER_BLOCK, PEER_HEADS, -1)
        cidx = (i_top[:, :, 0, :, None] * PEER_NK + i_top[:, :, 1, None, :]).reshape(PEER_BLOCK, PEER_HEADS, -1)
        f_s, f_i = lax.top_k(comb, PEER_TOPK)
        eidx = jnp.take_along_axis(cidx, f_i, axis=-1)
        gate = jax.nn.softmax(f_s, axis=-1).astype(xt.dtype)
        u = peer_u[eidx]
        v = peer_v[eidx]
        act = jax.nn.gelu(jnp.einsum('thkd,td->thk', u, xt))
        return jnp.einsum('thk,thkd->td', gate * act, v)

    out = lax.map(block, xb).reshape(-1, D)[:n]
    return out.reshape(B, T, D)


def layer(x, pool_hist, conv_hist, pos0, g_mix, w_in, b_in, w_pool, b_pool, pool_scale,
          w_dw, b_dw, ln_g, ln_b, w_out, g_ffn, w_query, sub_keys, peer_u, peer_v):
    h = rmsnorm(x, g_mix)
    z = h @ w_in + b_in
    za = z[..., :POOL_W]
    zv = z[..., POOL_W:POOL_W + CONV_W]
    zg = z[..., POOL_W + CONV_W:]
    a = pool_mixer(za, pool_hist, pos0, w_pool, b_pool, pool_scale)
    u = zv * jax.nn.sigmoid(zg)
    up = jnp.concatenate([conv_hist, u], axis=1)
    c = causal_depthwise_conv(up, w_dw) + b_dw
    c = jax.nn.silu(layernorm(c, ln_g, ln_b))
    x = x + jnp.concatenate([a, c], axis=-1) @ w_out
    x = x + peer(rmsnorm(x, g_ffn), w_query, sub_keys, peer_u, peer_v)
    new_pool = jnp.concatenate([pool_hist, za], axis=1)[:, -POOL_HIST:]
    new_conv = up[:, -CONV_HIST:]
    return x, new_pool, new_conv


def trunk(x, pool_hists, conv_hists, pos0, params, g_final):
    new_p, new_c = [], []
    for l in range(DEPTH):
        x, sp, sc = layer(x, pool_hists[l], conv_hists[l], pos0, *[p[l] for p in params])
        new_p.append(sp)
        new_c.append(sc)
    return rmsnorm(x, g_final), jnp.stack(new_p), jnp.stack(new_c)


def setup_inputs(seed: int = 0) -> dict:
    key = jax.random.key(seed)
    ks = jax.random.split(key, 24)
    nrm = lambda k, s, sc: jax.random.normal(k, s, jnp.float32) * sc
    L = DEPTH
    return {
        "x_prompt": nrm(ks[0], (BATCH, SEQ, D_MODEL), 1.0),
        "x_sample": nrm(ks[1], (DEC_BATCH, DEC_SEQ, D_MODEL), 1.0),
        "state_pool": nrm(ks[2], (L, DEC_BATCH, POOL_HIST, POOL_W), 1.0),
        "state_conv": nrm(ks[3], (L, DEC_BATCH, CONV_HIST, CONV_W), 0.5),
        "g_mix": 1.0 + nrm(ks[4], (L, D_MODEL), 0.02),
        "w_in": nrm(ks[5], (L, D_MODEL, IN_W), D_MODEL ** -0.5),
        "b_in": nrm(ks[6], (L, IN_W), 0.02),
        "w_pool": nrm(ks[7], (L, POOL_GROUPS, POOL_GW, POOL_GW), POOL_GW ** -0.5),
        "b_pool": nrm(ks[8], (L, POOL_GROUPS, POOL_GW), 0.02),
        "pool_scale": 1.0 + nrm(ks[9], (L, POOL_W), 0.02),
        "w_dw": nrm(ks[10], (L, CONV_K, CONV_W), CONV_K ** -0.5),
        "b_dw": nrm(ks[11], (L, CONV_W), 0.02),
        "ln_g": 1.0 + nrm(ks[12], (L, CONV_W), 0.02),
        "ln_b": nrm(ks[13], (L, CONV_W), 0.02),
        "w_out": nrm(ks[14], (L, MIX_W, D_MODEL), MIX_W ** -0.5),
        "g_ffn": 1.0 + nrm(ks[15], (L, D_MODEL), 0.02),
        "w_query": nrm(ks[16], (L, D_MODEL, PEER_HEADS * PEER_DK), D_MODEL ** -0.5),
        "sub_keys": nrm(ks[17], (L, PEER_HEADS, 2, PEER_NK, PEER_DKH), PEER_DKH ** -0.5),
        "peer_u": nrm(ks[18], (L, PEER_N, D_MODEL), D_MODEL ** -0.5),
        "peer_v": nrm(ks[19], (L, PEER_N, D_MODEL), 0.3),
        "g_final": 1.0 + nrm(ks[20], (D_MODEL,), 0.02),
    }


def reference(x_prompt, x_sample, state_pool, state_conv, g_mix, w_in, b_in, w_pool, b_pool,
              pool_scale, w_dw, b_dw, ln_g, ln_b, w_out, g_ffn, w_query, sub_keys, peer_u,
              peer_v, g_final):
    params = (g_mix, w_in, b_in, w_pool, b_pool, pool_scale, w_dw, b_dw, ln_g, ln_b,
              w_out, g_ffn, w_query, sub_keys, peer_u, peer_v)
    B = x_prompt.shape[0]
    zp = jnp.zeros((DEPTH, B, POOL_HIST, POOL_W), x_prompt.dtype)
    zc = jnp.zeros((DEPTH, B, CONV_HIST, CONV_W), x_prompt.dtype)
    y_prompt, new_pool_prompt, new_conv_prompt = trunk(x_prompt, zp, zc, 0, params, g_final)
    y_sample, new_pool_sample, new_conv_sample = trunk(x_sample, state_pool, state_conv, PAST_LEN, params, g_final)
    return (y_prompt, y_sample, new_pool_prompt, new_conv_prompt, new_pool_sample, new_conv_sample)
```

```python
import functools
import math

import jax
import jax.numpy as jnp
from jax import lax
from jax.experimental import pallas as pl
from jax.experimental.pallas import tpu as pltpu

D_MODEL = 1024
POOL_W = 512
POOL_WINDOWS = (2, 4, 8, 16)
POOL_GW = POOL_W // len(POOL_WINDOWS)
POOL_HIST = max(POOL_WINDOWS) - 1
CONV_W = 512
CONV_K = 31
CONV_HIST = CONV_K - 1
IN_W = POOL_W + 2 * CONV_W
PEER_HEADS = 8
PEER_NK = 128
PEER_DKH = 128
PEER_TOPK = 16
PEER_SLOTS = PEER_HEADS * PEER_TOPK
PAST_LEN = 1024
EPS = 1e-6

V7X_LANES = 128
V7X_SUBLANES = 8
V7X_VMEM_BYTES = 64 * 1024 * 1024

POOL_HALO = 16
CONV_HALO = 32

BLOCK_TOKENS = 256
GATHER_TOKENS = 8
PEER_BLOCK = 128

F32 = jnp.float32
BF16 = jnp.bfloat16


def _rms(x, g):
    return x * lax.rsqrt(jnp.mean(x * x, axis=-1, keepdims=True) + EPS) * g


def _sigmoid(x):
    return 1.0 / (1.0 + jnp.exp(-x))


def _gelu_tanh(x):
    c = math.sqrt(2.0 / math.pi)
    return 0.5 * x * (1.0 + jnp.tanh(c * (x + 0.044715 * (x * x * x))))


def _split_bf16(a):
    hi = a.astype(BF16)
    lo = (a - hi.astype(F32)).astype(BF16)
    return hi, lo


def _dot(a, b):
    return jnp.dot(a, b, preferred_element_type=F32)


def _dot_nt(a, b):
    return lax.dot_general(a, b, (((1,), (1,)), ((), ())), preferred_element_type=F32)


def _topk_rows(s, k):
    n, width = s.shape
    rows = lax.broadcasted_iota(jnp.int32, (n, width), 0).astype(F32)
    out_rows = lax.broadcasted_iota(jnp.int32, (k, width), 0)
    vals = jnp.zeros((k, width), F32)
    idxs = jnp.zeros((k, width), F32)
    for j in range(k):
        m = jnp.max(s, axis=0, keepdims=True)
        i = jnp.min(jnp.where(s == m, rows, float(n)), axis=0, keepdims=True)
        vals = jnp.where(out_rows == j, m, vals)
        idxs = jnp.where(out_rows == j, i, idxs)
        s = jnp.where(rows == i, -jnp.inf, s)
    return vals, idxs


def _select_row(table, row):
    rows = lax.broadcasted_iota(jnp.int32, table.shape, 0).astype(F32)
    return jnp.sum(jnp.where(rows == row, table, 0.0), axis=0, keepdims=True)


def _mixer_router_kernel(
        x_ref, ph_ref, ch_ref, g_mix_ref, w_in_ref, b_in_ref, w_pool_ref, b_pool_ref,
        pool_scale_ref, w_dw_ref, b_dw_ref, ln_g_ref, ln_b_ref, w_out_ref, g_ffn_ref,
        wq_hi_ref, wq_lo_ref, keys_hi_ref, keys_lo_ref,
        x1_ref, eidx_ref, gate_ref, new_pool_ref, new_conv_ref,
        zbuf, ubuf, s_scr, top_s, top_i, *, pos0, seqs, steps):
    tb = pl.program_id(1)
    tokens = seqs * steps

    @pl.when(tb == 0)
    def _():
        zbuf[:, POOL_HALO - POOL_HIST:POOL_HALO, :] = ph_ref[...]
        ubuf[:, CONV_HALO - CONV_HIST:CONV_HALO, :] = ch_ref[...]

    x = x_ref[...].reshape(tokens, D_MODEL)
    h = _rms(x, g_mix_ref[...])
    z = _dot(h.astype(BF16), w_in_ref[...]) + b_in_ref[...]
    za = z[:, :POOL_W]
    zv = z[:, POOL_W:POOL_W + CONV_W]
    zg = z[:, POOL_W + CONV_W:]
    u = zv * _sigmoid(zg)
    zbuf[:, POOL_HALO:, :] = za.reshape(seqs, steps, POOL_W)
    ubuf[:, CONV_HALO:, :] = u.reshape(seqs, steps, CONV_W)

    pos = pos0 + tb * steps + lax.broadcasted_iota(jnp.int32, (steps, 1), 0)

    a_parts, c_parts = [], []
    for s in range(seqs):
        groups = []
        for g, w in enumerate(POOL_WINDOWS):
            cols = slice(g * POOL_GW, (g + 1) * POOL_GW)
            tot = zbuf[s, POOL_HALO:POOL_HALO + steps, cols]
            for j in range(1, w):
                tot = tot + zbuf[s, POOL_HALO - j:POOL_HALO - j + steps, cols]
            cnt = jnp.minimum(pos + 1, w).astype(F32)
            d = tot / cnt - zbuf[s, POOL_HALO:POOL_HALO + steps, cols]
            groups.append(_dot(d.astype(BF16), w_pool_ref[g]))
        a_parts.append(jnp.concatenate(groups, axis=-1))

        conv = jnp.zeros((steps, CONV_W), F32)
        for k in range(CONV_K):
            lo = CONV_HALO - CONV_HIST + k
            conv = conv + ubuf[s, lo:lo + steps, :] * w_dw_ref[k:k + 1, :]
        c_parts.append(conv)

    a = jnp.concatenate(a_parts, axis=0) if seqs > 1 else a_parts[0]
    a = (a + b_pool_ref[...]) * pool_scale_ref[...]
    c = jnp.concatenate(c_parts, axis=0) if seqs > 1 else c_parts[0]
    c = c + b_dw_ref[...]
    mu = jnp.mean(c, axis=-1, keepdims=True)
    var = jnp.mean(jnp.square(c - mu), axis=-1, keepdims=True)
    c = (c - mu) * lax.rsqrt(var + EPS) * ln_g_ref[...] + ln_b_ref[...]
    c = c * _sigmoid(c)

    x1 = (x + _dot(a.astype(BF16), w_out_ref[:POOL_W, :])
          + _dot(c.astype(BF16), w_out_ref[POOL_W:, :]))
    x1_ref[...] = x1.reshape(seqs, steps, D_MODEL)

    new_pool_ref[...] = zbuf[:, steps + POOL_HALO - POOL_HIST:steps + POOL_HALO, :]
    new_conv_ref[...] = ubuf[:, steps + CONV_HALO - CONV_HIST:steps + CONV_HALO, :]
    zbuf[:, POOL_HALO - POOL_HIST:POOL_HALO, :] = (
        zbuf[:, steps + POOL_HALO - POOL_HIST:steps + POOL_HALO, :])
    ubuf[:, CONV_HALO - CONV_HIST:CONV_HALO, :] = (
        ubuf[:, steps + CONV_HALO - CONV_HIST:steps + CONV_HALO, :])

    h2 = _rms(x1, g_ffn_ref[...])
    h2_hi, h2_lo = _split_bf16(h2)
    q = (_dot(h2_hi, wq_hi_ref[...]) + _dot(h2_lo, wq_hi_ref[...])
         + _dot(h2_hi, wq_lo_ref[...]))
    for l in range(2 * PEER_HEADS):
        q_hi, q_lo = _split_bf16(q[:, l * PEER_DKH:(l + 1) * PEER_DKH])
        k_hi = keys_hi_ref[l]
        k_lo = keys_lo_ref[l]
        s_scr[l] = _dot_nt(k_hi, q_hi) + _dot_nt(k_hi, q_lo) + _dot_nt(k_lo, q_hi)

    lane_groups = tokens // V7X_LANES

    def stage1(it, carry):
        l = it // lane_groups
        lanes = pl.ds(pl.multiple_of((it % lane_groups) * V7X_LANES, V7X_LANES), V7X_LANES)
        vals, idxs = _topk_rows(s_scr[l, :, lanes], PEER_TOPK)
        top_s[l, :, lanes] = vals
        top_i[l, :, lanes] = idxs
        return carry

    lax.fori_loop(0, 2 * PEER_HEADS * lane_groups, stage1, 0)

    def stage2(it, carry):
        hd = it // lane_groups
        lanes = pl.ds(pl.multiple_of((it % lane_groups) * V7X_LANES, V7X_LANES), V7X_LANES)
        s1 = top_s[2 * hd, :, lanes]
        s2 = top_s[2 * hd + 1, :, lanes]
        i1 = top_i[2 * hd, :, lanes]
        i2 = top_i[2 * hd + 1, :, lanes]
        comb = jnp.concatenate([s1[r:r + 1, :] + s2 for r in range(PEER_TOPK)], axis=0)
        f_s, f_i = _topk_rows(comb, PEER_TOPK)
        out_rows = lax.broadcasted_iota(jnp.int32, (PEER_TOPK, V7X_LANES), 0)
        eidx = jnp.zeros((PEER_TOPK, V7X_LANES), F32)
        for j in range(PEER_TOPK):
            fj = f_i[j:j + 1, :]
            ra = jnp.floor(fj * (1.0 / PEER_TOPK))
            rb = fj - ra * PEER_TOPK
            e = _select_row(i1, ra) * PEER_NK + _select_row(i2, rb)
            eidx = jnp.where(out_rows == j, e, eidx)
        p = jnp.exp(f_s - jnp.max(f_s, axis=0, keepdims=True))
        gate = p / jnp.sum(p, axis=0, keepdims=True)
        rows = pl.ds(pl.multiple_of(hd * PEER_TOPK, PEER_TOPK), PEER_TOPK)
        eidx_ref[rows, lanes] = eidx.astype(jnp.int32)
        gate_ref[rows, lanes] = gate
        return carry

    lax.fori_loop(0, PEER_HEADS * lane_groups, stage2, 0)


def _const_spec(shape):
    zeros = (0,) * len(shape)
    return pl.BlockSpec(shape, lambda i, j: zeros, pipeline_mode=pl.Buffered(1))


def _mixer_router(x, pool_hist, conv_hist, pos0, seqs, steps, weights):
    nseq, length, _ = x.shape
    tokens = seqs * steps
    grid = (nseq // seqs, length // steps)
    per_time = pl.BlockSpec((seqs, steps, D_MODEL), lambda i, j: (i, j, 0))
    per_seq_pool = pl.BlockSpec((seqs, POOL_HIST, POOL_W), lambda i, j: (i, 0, 0))
    per_seq_conv = pl.BlockSpec((seqs, CONV_HIST, CONV_W), lambda i, j: (i, 0, 0))
    blocks_per_seq_group = length // steps
    slot_spec = pl.BlockSpec(
        (PEER_SLOTS, tokens), lambda i, j: (0, i * blocks_per_seq_group + j))
    n_tok = nseq * length
    kernel = functools.partial(_mixer_router_kernel, pos0=pos0, seqs=seqs, steps=steps)
    return pl.pallas_call(
        kernel,
        grid=grid,
        in_specs=[per_time, per_seq_pool, per_seq_conv] + [_const_spec(w.shape) for w in weights],
        out_specs=[per_time, slot_spec, slot_spec, per_seq_pool, per_seq_conv],
        out_shape=[
            jax.ShapeDtypeStruct((nseq, length, D_MODEL), F32),
            jax.ShapeDtypeStruct((PEER_SLOTS, n_tok), jnp.int32),
            jax.ShapeDtypeStruct((PEER_SLOTS, n_tok), F32),
            jax.ShapeDtypeStruct((nseq, POOL_HIST, POOL_W), F32),
            jax.ShapeDtypeStruct((nseq, CONV_HIST, CONV_W), F32),
        ],
        scratch_shapes=[
            pltpu.VMEM((seqs, POOL_HALO + steps, POOL_W), F32),
            pltpu.VMEM((seqs, CONV_HALO + steps, CONV_W), F32),
            pltpu.VMEM((2 * PEER_HEADS, PEER_NK, tokens), F32),
            pltpu.VMEM((2 * PEER_HEADS, PEER_TOPK, tokens), F32),
            pltpu.VMEM((2 * PEER_HEADS, PEER_TOPK, tokens), F32),
        ],
        compiler_params=pltpu.CompilerParams(
            dimension_semantics=("arbitrary", "arbitrary"),
            vmem_limit_bytes=V7X_VMEM_BYTES * 3 // 4),
        name="mixer_router",
    )(x, pool_hist, conv_hist, *weights)


def _peer_experts_kernel(eidx_ref, x1_ref, gate_ref, g_ffn_ref, g_final_ref, u_hbm, v_hbm,
                         y_ref, ubuf, vbuf, sems):
    batches = PEER_BLOCK // GATHER_TOKENS
    rows_per_batch = GATHER_TOKENS * PEER_SLOTS

    def gather(batch, slot, start):
        def body(r, carry):
            t = batch * GATHER_TOKENS + r // PEER_SLOTS
            e = eidx_ref[t, r % PEER_SLOTS]
            cu = pltpu.make_async_copy(u_hbm.at[pl.ds(e, 1)], ubuf.at[slot, pl.ds(r, 1)],
                                       sems.at[0, slot])
            cv = pltpu.make_async_copy(v_hbm.at[pl.ds(e, 1)], vbuf.at[slot, pl.ds(r, 1)],
                                       sems.at[1, slot])
            cu.start()
            cv.start()
            return carry

        if start:
            lax.fori_loop(0, rows_per_batch, body, 0)
        else:
            pltpu.make_async_copy(u_hbm.at[pl.ds(0, rows_per_batch)], ubuf.at[slot],
                                  sems.at[0, slot]).wait()
            pltpu.make_async_copy(v_hbm.at[pl.ds(0, rows_per_batch)], vbuf.at[slot],
                                  sems.at[1, slot]).wait()

    gather(0, 0, True)
    g_ffn = g_ffn_ref[...]
    g_final = g_final_ref[...]
    gate_blk = gate_ref[...]
    lane = lax.broadcasted_iota(jnp.int32, (PEER_SLOTS, PEER_BLOCK), 1)

    def batch_body(b, carry):
        slot = b % 2

        @pl.when(b + 1 < batches)
        def _():
            gather(b + 1, 1 - slot, True)

        gather(b, slot, False)
        for r in range(GATHER_TOKENS):
            t = b * GATHER_TOKENS + r
            x1 = x1_ref[pl.ds(t, 1), :]
            h2 = _rms(x1, g_ffn)
            rows = pl.ds(r * PEER_SLOTS, PEER_SLOTS)
            act = jnp.sum(ubuf[slot, rows, :] * h2, axis=-1, keepdims=True)
            gate = jnp.sum(jnp.where(lane == t, gate_blk, 0.0), axis=-1, keepdims=True)
            coef = gate * _gelu_tanh(act)
            out = jnp.sum(vbuf[slot, rows, :] * coef, axis=0, keepdims=True)
            y_ref[pl.ds(t, 1), :] = _rms(x1 + out, g_final)
        return carry

    lax.fori_loop(0, batches, batch_body, 0)


def _peer_experts(eidx_rows, x1, gate_t, g_ffn, g_final, peer_u, peer_v):
    n_tok = x1.shape[0]
    rows_per_batch = GATHER_TOKENS * PEER_SLOTS
    return pl.pallas_call(
        _peer_experts_kernel,
        grid=(n_tok // PEER_BLOCK,),
        in_specs=[
            pl.BlockSpec((PEER_BLOCK, PEER_SLOTS), lambda i: (i, 0),
                         memory_space=pltpu.SMEM),
            pl.BlockSpec((PEER_BLOCK, D_MODEL), lambda i: (i, 0)),
            pl.BlockSpec((PEER_SLOTS, PEER_BLOCK), lambda i: (0, i)),
            pl.BlockSpec((1, D_MODEL), lambda i: (0, 0)),
            pl.BlockSpec((1, D_MODEL), lambda i: (0, 0)),
            pl.BlockSpec(memory_space=pl.ANY),
            pl.BlockSpec(memory_space=pl.ANY),
        ],
        out_specs=pl.BlockSpec((PEER_BLOCK, D_MODEL), lambda i: (i, 0)),
        out_shape=jax.ShapeDtypeStruct((n_tok, D_MODEL), F32),
        scratch_shapes=[
            pltpu.VMEM((2, rows_per_batch, D_MODEL), F32),
            pltpu.VMEM((2, rows_per_batch, D_MODEL), F32),
            pltpu.SemaphoreType.DMA((2, 2)),
        ],
        compiler_params=pltpu.CompilerParams(
            dimension_semantics=("arbitrary",),
            vmem_limit_bytes=V7X_VMEM_BYTES // 2),
        name="peer_experts",
    )(eidx_rows, x1, gate_t, g_ffn, g_final, peer_u, peer_v)


def _trunk(x, pool_hist, conv_hist, pos0, seqs, steps, weights, g_ffn, g_final, peer_u, peer_v):
    nseq, length, _ = x.shape
    x1, eidx_t, gate_t, new_pool, new_conv = _mixer_router(
        x, pool_hist, conv_hist, pos0, seqs, steps, weights)
    y = _peer_experts(eidx_t.T, x1.reshape(nseq * length, D_MODEL), gate_t,
                      g_ffn, g_final, peer_u, peer_v)
    return y.reshape(nseq, length, D_MODEL), new_pool[None], new_conv[None]


def kernel(x_prompt, x_sample, state_pool, state_conv, g_mix, w_in, b_in, w_pool, b_pool,
           pool_scale, w_dw, b_dw, ln_g, ln_b, w_out, g_ffn, w_query, sub_keys, peer_u,
           peer_v, g_final):
    assert g_mix.shape[0] == 1, "single-layer trunk"
    wq_hi, wq_lo = _split_bf16(w_query[0])
    keys = sub_keys[0].reshape(2 * PEER_HEADS, PEER_NK, PEER_DKH)
    keys_hi, keys_lo = _split_bf16(keys)
    row = lambda v: v.reshape(1, -1)
    g_ffn_row = row(g_ffn[0])
    weights = (
        row(g_mix[0]), w_in[0].astype(BF16), row(b_in[0]), w_pool[0].astype(BF16),
        row(b_pool[0]), row(pool_scale[0]), w_dw[0], row(b_dw[0]), row(ln_g[0]),
        row(ln_b[0]), w_out[0].astype(BF16), g_ffn_row, wq_hi, wq_lo, keys_hi, keys_lo)
    g_final_row = row(g_final)

    n_prompt = x_prompt.shape[0]
    zero_pool = jnp.zeros((n_prompt, POOL_HIST, POOL_W), F32)
    zero_conv = jnp.zeros((n_prompt, CONV_HIST, CONV_W), F32)
    y_p, pool_p, conv_p = _trunk(
        x_prompt, zero_pool, zero_conv, 0, 1, BLOCK_TOKENS, weights, g_ffn_row,
        g_final_row, peer_u[0], peer_v[0])
    dec_seq = x_sample.shape[1]
    y_s, pool_s, conv_s = _trunk(
        x_sample, state_pool[0], state_conv[0], PAST_LEN, BLOCK_TOKENS // dec_seq, dec_seq,
        weights, g_ffn_row, g_final_row, peer_u[0], peer_v[0])
    return (y_p, y_s, pool_p, conv_p, pool_s, conv_s)
```

```python
import functools
import math

import jax
import jax.numpy as jnp
from jax import lax
from jax.experimental import pallas as pl
from jax.experimental.pallas import tpu as pltpu

D_MODEL = 1024
POOL_W = 512
POOL_WINDOWS = (2, 4, 8, 16)
POOL_GW = POOL_W // len(POOL_WINDOWS)
POOL_HIST = max(POOL_WINDOWS) - 1
CONV_W = 512
CONV_K = 31
CONV_HIST = CONV_K - 1
IN_W = POOL_W + 2 * CONV_W
PEER_HEADS = 8
PEER_NK = 128
PEER_N = PEER_NK * PEER_NK
PEER_DKH = 128
PEER_TOPK = 16
PEER_SLOTS = PEER_HEADS * PEER_TOPK
PAST_LEN = 1024
EPS = 1e-6

V7X_LANES = 128
V7X_SUBLANES = 8
V7X_VMEM_BYTES = 64 * 1024 * 1024

POOL_HALO = 16
CONV_HALO = 32

BLOCK_TOKENS = 256
PEER_BLOCK = 128
NORM_BLOCK = 512

ROW_TILE = D_MODEL // V7X_LANES
PEER_CHUNKS = 4
CHUNK_EXPERTS = PEER_N // PEER_CHUNKS
GROUP = 2 * V7X_SUBLANES
LIST_ROWS = 192
META_ROWS = 256
GSTART_ROW = LIST_ROWS
SLAB_ROWS = 256
assert LIST_ROWS >= PEER_SLOTS + PEER_CHUNKS * (GROUP - 1)
assert ROW_TILE == V7X_SUBLANES

F32 = jnp.float32
BF16 = jnp.bfloat16


def _rms(x, g):
    return x * lax.rsqrt(jnp.mean(x * x, axis=-1, keepdims=True) + EPS) * g


def _sigmoid(x):
    return 1.0 / (1.0 + jnp.exp(-x))


def _gelu_tanh(x):
    c = math.sqrt(2.0 / math.pi)
    return 0.5 * x * (1.0 + jnp.tanh(c * (x + 0.044715 * (x * x * x))))


def _split_bf16(a):
    hi = a.astype(BF16)
    lo = (a - hi.astype(F32)).astype(BF16)
    return hi, lo


def _dot(a, b):
    return jnp.dot(a, b, preferred_element_type=F32)


def _dot_nt(a, b):
    return lax.dot_general(a, b, (((1,), (1,)), ((), ())), preferred_element_type=F32)


def _row_iota(shape):
    return lax.broadcasted_iota(jnp.int32, shape, 0).astype(F32)


def _topk_rows(s, k):
    n, width = s.shape
    rows = _row_iota((n, width))
    out_rows = lax.broadcasted_iota(jnp.int32, (k, width), 0)
    vals = jnp.zeros((k, width), F32)
    idxs = jnp.zeros((k, width), F32)
    for j in range(k):
        m = jnp.max(s, axis=0, keepdims=True)
        i = jnp.min(jnp.where(s == m, rows, float(n)), axis=0, keepdims=True)
        vals = jnp.where(out_rows == j, m, vals)
        idxs = jnp.where(out_rows == j, i, idxs)
        s = jnp.where(rows == i, -jnp.inf, s)
    return vals, idxs


def _select_row(table, row):
    return jnp.sum(jnp.where(_row_iota(table.shape) == row, table, 0.0), axis=0, keepdims=True)


def _mixer_router_kernel(
        x_ref, ph_ref, ch_ref, g_mix_ref, w_in_ref, b_in_ref, w_pool_ref, b_pool_ref,
        pool_scale_ref, w_dw_ref, b_dw_ref, ln_g_ref, ln_b_ref, w_out_ref, g_ffn_ref,
        wq_hi_ref, wq_lo_ref, keys_hi_ref, keys_lo_ref,
        x1_ref, h2_ref, meta_ref, pgate_ref, new_pool_ref, new_conv_ref,
        zbuf, ubuf, s_scr, top_s, top_i, e_scr, g_scr, key_scr, p_scr, off_scr,
        *, pos0, seqs, steps):
    tb = pl.program_id(1)
    tokens = seqs * steps

    @pl.when(tb == 0)
    def _():
        zbuf[:, POOL_HALO - POOL_HIST:POOL_HALO, :] = ph_ref[...]
        ubuf[:, CONV_HALO - CONV_HIST:CONV_HALO, :] = ch_ref[...]

    x = x_ref[...].reshape(tokens, D_MODEL)
    h = _rms(x, g_mix_ref[...])
    z = _dot(h.astype(BF16), w_in_ref[...]) + b_in_ref[...]
    za = z[:, :POOL_W]
    zv = z[:, POOL_W:POOL_W + CONV_W]
    zg = z[:, POOL_W + CONV_W:]
    u = zv * _sigmoid(zg)
    zbuf[:, POOL_HALO:, :] = za.reshape(seqs, steps, POOL_W)
    ubuf[:, CONV_HALO:, :] = u.reshape(seqs, steps, CONV_W)

    pos = pos0 + tb * steps + lax.broadcasted_iota(jnp.int32, (steps, 1), 0)

    a_parts, c_parts = [], []
    for s in range(seqs):
        groups = []
        for g, w in enumerate(POOL_WINDOWS):
            cols = slice(g * POOL_GW, (g + 1) * POOL_GW)
            tot = zbuf[s, POOL_HALO:POOL_HALO + steps, cols]
            for j in range(1, w):
                tot = tot + zbuf[s, POOL_HALO - j:POOL_HALO - j + steps, cols]
            cnt = jnp.minimum(pos + 1, w).astype(F32)
            d = tot / cnt - zbuf[s, POOL_HALO:POOL_HALO + steps, cols]
            groups.append(_dot(d.astype(BF16), w_pool_ref[g]))
        a_parts.append(jnp.concatenate(groups, axis=-1))

        conv = jnp.zeros((steps, CONV_W), F32)
        for k in range(CONV_K):
            lo = CONV_HALO - CONV_HIST + k
            conv = conv + ubuf[s, lo:lo + steps, :] * w_dw_ref[k:k + 1, :]
        c_parts.append(conv)

    a = jnp.concatenate(a_parts, axis=0) if seqs > 1 else a_parts[0]
    a = (a + b_pool_ref[...]) * pool_scale_ref[...]
    c = jnp.concatenate(c_parts, axis=0) if seqs > 1 else c_parts[0]
    c = c + b_dw_ref[...]
    mu = jnp.mean(c, axis=-1, keepdims=True)
    var = jnp.mean(jnp.square(c - mu), axis=-1, keepdims=True)
    c = (c - mu) * lax.rsqrt(var + EPS) * ln_g_ref[...] + ln_b_ref[...]
    c = c * _sigmoid(c)

    x1 = (x + _dot(a.astype(BF16), w_out_ref[:POOL_W, :])
          + _dot(c.astype(BF16), w_out_ref[POOL_W:, :]))
    x1_ref[...] = x1.reshape(seqs, steps, D_MODEL)

    new_pool_ref[...] = zbuf[:, steps + POOL_HALO - POOL_HIST:steps + POOL_HALO, :]
    new_conv_ref[...] = ubuf[:, steps + CONV_HALO - CONV_HIST:steps + CONV_HALO, :]
    zbuf[:, POOL_HALO - POOL_HIST:POOL_HALO, :] = (
        zbuf[:, steps + POOL_HALO - POOL_HIST:steps + POOL_HALO, :])
    ubuf[:, CONV_HALO - CONV_HIST:CONV_HALO, :] = (
        ubuf[:, steps + CONV_HALO - CONV_HIST:steps + CONV_HALO, :])

    h2 = _rms(x1, g_ffn_ref[...])
    h2_ref[...] = h2
    h2_hi, h2_lo = _split_bf16(h2)
    q = (_dot(h2_hi, wq_hi_ref[...]) + _dot(h2_lo, wq_hi_ref[...])
         + _dot(h2_hi, wq_lo_ref[...]))
    for l in range(2 * PEER_HEADS):
        q_hi, q_lo = _split_bf16(q[:, l * PEER_DKH:(l + 1) * PEER_DKH])
        k_hi = keys_hi_ref[l]
        k_lo = keys_lo_ref[l]
        s_scr[l] = _dot_nt(k_hi, q_hi) + _dot_nt(k_hi, q_lo) + _dot_nt(k_lo, q_hi)

    lane_groups = tokens // V7X_LANES

    def lanes_of(it):
        return pl.ds(pl.multiple_of((it % lane_groups) * V7X_LANES, V7X_LANES), V7X_LANES)

    def stage1(it, carry):
        l = it // lane_groups
        lanes = lanes_of(it)
        vals, idxs = _topk_rows(s_scr[l, :, lanes], PEER_TOPK)
        top_s[l, :, lanes] = vals
        top_i[l, :, lanes] = idxs
        return carry

    lax.fori_loop(0, 2 * PEER_HEADS * lane_groups, stage1, 0)

    def stage2(it, carry):
        hd = it // lane_groups
        lanes = lanes_of(it)
        s1 = top_s[2 * hd, :, lanes]
        s2 = top_s[2 * hd + 1, :, lanes]
        i1 = top_i[2 * hd, :, lanes]
        i2 = top_i[2 * hd + 1, :, lanes]
        comb = jnp.concatenate([s1[r:r + 1, :] + s2 for r in range(PEER_TOPK)], axis=0)
        f_s, f_i = _topk_rows(comb, PEER_TOPK)
        out_rows = lax.broadcasted_iota(jnp.int32, (PEER_TOPK, V7X_LANES), 0)
        eidx = jnp.zeros((PEER_TOPK, V7X_LANES), F32)
        for j in range(PEER_TOPK):
            fj = f_i[j:j + 1, :]
            ra = jnp.floor(fj * (1.0 / PEER_TOPK))
            rb = fj - ra * PEER_TOPK
            e = _select_row(i1, ra) * PEER_NK + _select_row(i2, rb)
            eidx = jnp.where(out_rows == j, e, eidx)
        p = jnp.exp(f_s - jnp.max(f_s, axis=0, keepdims=True))
        gate = p / jnp.sum(p, axis=0, keepdims=True)
        rows = pl.ds(pl.multiple_of(hd * PEER_TOPK, PEER_TOPK), PEER_TOPK)
        e_scr[rows, lanes] = eidx
        g_scr[rows, lanes] = gate
        return carry

    lax.fori_loop(0, PEER_HEADS * lane_groups, stage2, 0)

    def stage3(lg):
        lanes = slice(lg * V7X_LANES, (lg + 1) * V7X_LANES)
        e = e_scr[:, lanes]
        slot = _row_iota((PEER_SLOTS, V7X_LANES))
        key = e * PEER_SLOTS + slot
        key_scr[:, lanes] = key

        def tile_of(ref, kb):
            return ref[pl.ds(pl.multiple_of(kb * V7X_SUBLANES, V7X_SUBLANES), V7X_SUBLANES), lanes]

        def rank_body(kb, rank):
            keys8 = tile_of(key_scr, kb)
            for j in range(V7X_SUBLANES):
                rank = rank + jnp.where(keys8[j:j + 1, :] < key, 1.0, 0.0)
            return rank

        rank = lax.fori_loop(0, PEER_SLOTS // V7X_SUBLANES, rank_body, jnp.zeros_like(key))
        chunk = jnp.floor(e * (1.0 / CHUNK_EXPERTS))
        start = jnp.zeros((1, V7X_LANES), F32)
        pstart = jnp.zeros((1, V7X_LANES), F32)
        shift = jnp.zeros_like(key)
        grow = lax.broadcasted_iota(jnp.int32, (V7X_SUBLANES, V7X_LANES), 0)
        gstart = jnp.zeros((V7X_SUBLANES, V7X_LANES), F32)
        for ch in range(PEER_CHUNKS):
            mine = chunk == float(ch)
            n = jnp.sum(jnp.where(mine, 1.0, 0.0), axis=0, keepdims=True)
            shift = jnp.where(mine, pstart - start, shift)
            start = start + n
            pstart = pstart + jnp.ceil(n * (1.0 / GROUP)) * GROUP
            gstart = jnp.where(grow == ch + 1, pstart * (1.0 / GROUP), gstart)
        p_scr[:, lanes] = rank + shift
        off_scr[:, lanes] = (e - chunk * CHUNK_EXPERTS) * ROW_TILE

        list_rows = _row_iota((LIST_ROWS, V7X_LANES))

        def place(kb, lists):
            offs, gates = lists
            p8, off8, gate8 = tile_of(p_scr, kb), tile_of(off_scr, kb), tile_of(g_scr, kb)
            for j in range(V7X_SUBLANES):
                hit = list_rows == p8[j:j + 1, :]
                offs = jnp.where(hit, off8[j:j + 1, :], offs)
                gates = jnp.where(hit, gate8[j:j + 1, :], gates)
            return offs, gates

        zeros = jnp.zeros((LIST_ROWS, V7X_LANES), F32)
        offs, gates = lax.fori_loop(0, PEER_SLOTS // V7X_SUBLANES, place, (zeros, zeros))
        meta_ref[0:LIST_ROWS, lanes] = offs.astype(jnp.int32)
        meta_ref[GSTART_ROW:GSTART_ROW + V7X_SUBLANES, lanes] = gstart.astype(jnp.int32)
        meta_ref[GSTART_ROW + V7X_SUBLANES:, lanes] = jnp.zeros(
            (META_ROWS - GSTART_ROW - V7X_SUBLANES, V7X_LANES), jnp.int32)
        pgate_ref[:, lanes] = gates

    for lg in range(lane_groups):
        stage3(lg)


def _const_spec(shape):
    zeros = (0,) * len(shape)
    return pl.BlockSpec(shape, lambda i, j: zeros, pipeline_mode=pl.Buffered(1))


def _mixer_router(x, pool_hist, conv_hist, pos0, seqs, steps, weights):
    nseq, length, _ = x.shape
    tokens = seqs * steps
    grid = (nseq // seqs, length // steps)
    blocks_per_seq_group = length // steps
    per_time = pl.BlockSpec((seqs, steps, D_MODEL), lambda i, j: (i, j, 0))
    per_token = pl.BlockSpec((tokens, D_MODEL), lambda i, j: (i * blocks_per_seq_group + j, 0))
    per_seq_pool = pl.BlockSpec((seqs, POOL_HIST, POOL_W), lambda i, j: (i, 0, 0))
    per_seq_conv = pl.BlockSpec((seqs, CONV_HIST, CONV_W), lambda i, j: (i, 0, 0))

    def slot_spec(rows):
        return pl.BlockSpec((rows, tokens), lambda i, j: (0, i * blocks_per_seq_group + j))

    n_tok = nseq * length
    kernel = functools.partial(_mixer_router_kernel, pos0=pos0, seqs=seqs, steps=steps)
    slot_scratch = pltpu.VMEM((PEER_SLOTS, tokens), F32)
    return pl.pallas_call(
        kernel,
        grid=grid,
        in_specs=[per_time, per_seq_pool, per_seq_conv] + [_const_spec(w.shape) for w in weights],
        out_specs=[per_time, per_token, slot_spec(META_ROWS), slot_spec(LIST_ROWS),
                   per_seq_pool, per_seq_conv],
        out_shape=[
            jax.ShapeDtypeStruct((nseq, length, D_MODEL), F32),
            jax.ShapeDtypeStruct((n_tok, D_MODEL), F32),
            jax.ShapeDtypeStruct((META_ROWS, n_tok), jnp.int32),
            jax.ShapeDtypeStruct((LIST_ROWS, n_tok), F32),
            jax.ShapeDtypeStruct((nseq, POOL_HIST, POOL_W), F32),
            jax.ShapeDtypeStruct((nseq, CONV_HIST, CONV_W), F32),
        ],
        scratch_shapes=[
            pltpu.VMEM((seqs, POOL_HALO + steps, POOL_W), F32),
            pltpu.VMEM((seqs, CONV_HALO + steps, CONV_W), F32),
            pltpu.VMEM((2 * PEER_HEADS, PEER_NK, tokens), F32),
            pltpu.VMEM((2 * PEER_HEADS, PEER_TOPK, tokens), F32),
            pltpu.VMEM((2 * PEER_HEADS, PEER_TOPK, tokens), F32),
            slot_scratch, slot_scratch, slot_scratch, slot_scratch, slot_scratch,
        ],
        compiler_params=pltpu.CompilerParams(
            dimension_semantics=("arbitrary", "arbitrary"),
            vmem_limit_bytes=V7X_VMEM_BYTES * 3 // 4),
        name="mixer_router",
    )(x, pool_hist, conv_hist, *weights)


def _tile_rows(index):
    return pl.ds(pl.multiple_of(index * ROW_TILE, ROW_TILE), ROW_TILE)


def _sum_tiles(p):
    sub = lax.broadcasted_iota(jnp.int32, (V7X_SUBLANES, V7X_LANES), 0)
    m4 = sub < 4
    m2 = (sub & 2) == 0
    m1 = (sub & 1) == 0
    b = [jnp.where(m4, p[i], p[i + 4]) + pltpu.roll(jnp.where(m4, p[i + 4], p[i]), 4, 0)
         for i in range(4)]
    c = [jnp.where(m2, b[i], pltpu.roll(b[i + 2], 2, 0))
         + jnp.where(m2, pltpu.roll(b[i], 6, 0), b[i + 2]) for i in range(2)]
    return (jnp.where(m1, c[0], pltpu.roll(c[1], 1, 0))
            + jnp.where(m1, pltpu.roll(c[0], 7, 0), c[1]))


def _tree_sum(terms):
    while len(terms) > 1:
        terms = [terms[i] + terms[i + 1] for i in range(0, len(terms), 2)]
    return terms[0]


def _peer_experts_kernel(meta_ref, pgate_ref, h2_ref, acc_ref, u_ref, v_ref, out_ref, pair_scr):
    chunk = pl.program_id(0)

    @pl.when((chunk == 0) & (pl.program_id(1) == 0))
    def _():
        pair_scr[...] = jnp.zeros(pair_scr.shape, F32)

    def group_range(t):
        bounds = t * META_ROWS + GSTART_ROW + chunk
        return meta_ref[bounds], meta_ref[bounds + 1]

    def dots_token(t, count):
        g0, g1 = group_range(t)
        h = h2_ref[_tile_rows(t), :]

        def dots_group(g, count):
            entry = t * META_ROWS + g * GROUP
            for half in range(GROUP // V7X_SUBLANES):
                prods = []
                for j in range(V7X_SUBLANES):
                    off = meta_ref[entry + half * V7X_SUBLANES + j]
                    prods.append(u_ref[pl.ds(pl.multiple_of(off, ROW_TILE), ROW_TILE), :] * h)
                pair_scr[_tile_rows(count * 2 + half), :] = _sum_tiles(prods)
            return count + 1

        return lax.fori_loop(g0, g1, dots_group, count)

    n_groups = lax.fori_loop(0, PEER_BLOCK, dots_token, 0)

    def activate(i, carry):
        rows = pl.ds(pl.multiple_of(i * SLAB_ROWS, SLAB_ROWS), SLAB_ROWS)
        act = jnp.sum(pair_scr[rows, :], axis=-1, keepdims=True)
        pair_scr[rows, :] = jnp.broadcast_to(_gelu_tanh(act), (SLAB_ROWS, V7X_LANES))
        return carry

    lax.fori_loop(0, (n_groups * GROUP + SLAB_ROWS - 1) // SLAB_ROWS, activate, 0)

    def mix_token(t, count):
        g0, g1 = group_range(t)

        def mix_group(g, carry):
            count, acc = carry
            entry = t * META_ROWS + g * GROUP
            gate_entry = t * LIST_ROWS + g * GROUP
            terms = []
            for half in range(GROUP // V7X_SUBLANES):
                coef = pair_scr[_tile_rows(count * 2 + half), :]
                for j in range(V7X_SUBLANES):
                    k = half * V7X_SUBLANES + j
                    off = meta_ref[entry + k]
                    cj = jnp.broadcast_to(coef[j:j + 1, :], (V7X_SUBLANES, V7X_LANES))
                    v_row = v_ref[pl.ds(pl.multiple_of(off, ROW_TILE), ROW_TILE), :]
                    terms.append((cj * pgate_ref[gate_entry + k]) * v_row)
            return count + 1, acc + _tree_sum(terms)

        count, acc = lax.fori_loop(g0, g1, mix_group, (count, acc_ref[_tile_rows(t), :]))
        out_ref[_tile_rows(t), :] = acc
        return count

    lax.fori_loop(0, PEER_BLOCK, mix_token, 0)


def _peer_experts(meta, pgate, h2_tiles, peer_u, peer_v):
    n_tok = meta.shape[0] // META_ROWS
    tok_spec = pl.BlockSpec((PEER_BLOCK * ROW_TILE, V7X_LANES), lambda c, i: (i, 0))
    table_spec = pl.BlockSpec((CHUNK_EXPERTS * ROW_TILE, V7X_LANES), lambda c, i: (c, 0),
                              pipeline_mode=pl.Buffered(1))
    pair_rows = PEER_BLOCK * PEER_SLOTS
    table_bytes = 2 * CHUNK_EXPERTS * D_MODEL * 4
    block_bytes = PEER_BLOCK * D_MODEL * 4
    vmem_bytes = table_bytes + pair_rows * V7X_LANES * 4 + 6 * block_bytes + (4 << 20)
    assert vmem_bytes <= V7X_VMEM_BYTES
    return pl.pallas_call(
        _peer_experts_kernel,
        grid=(PEER_CHUNKS, n_tok // PEER_BLOCK),
        in_specs=[
            pl.BlockSpec((PEER_BLOCK * META_ROWS,), lambda c, i: (i,), memory_space=pltpu.SMEM),
            pl.BlockSpec((PEER_BLOCK * LIST_ROWS,), lambda c, i: (i,), memory_space=pltpu.SMEM),
            tok_spec, tok_spec, table_spec, table_spec,
        ],
        out_specs=tok_spec,
        out_shape=jax.ShapeDtypeStruct((n_tok * ROW_TILE, V7X_LANES), F32),
        scratch_shapes=[pltpu.VMEM((pair_rows, V7X_LANES), F32)],
        input_output_aliases={3: 0},
        compiler_params=pltpu.CompilerParams(
            dimension_semantics=("arbitrary", "arbitrary"),
            vmem_limit_bytes=vmem_bytes),
        name="peer_experts",
    )(meta, pgate, h2_tiles, jnp.zeros((n_tok * ROW_TILE, V7X_LANES), F32), peer_u, peer_v)


def _final_norm_kernel(x1_ref, peer_ref, g_ref, y_ref):
    y_ref[...] = _rms(x1_ref[...] + peer_ref[...], g_ref[...])


def _final_norm(x1, peer_out, g_final):
    n_tok = x1.shape[0]
    spec = pl.BlockSpec((NORM_BLOCK, D_MODEL), lambda i: (i, 0))
    return pl.pallas_call(
        _final_norm_kernel,
        grid=(n_tok // NORM_BLOCK,),
        in_specs=[spec, spec, pl.BlockSpec((1, D_MODEL), lambda i: (0, 0))],
        out_specs=spec,
        out_shape=jax.ShapeDtypeStruct((n_tok, D_MODEL), F32),
        name="final_norm",
    )(x1, peer_out, g_final)


def _trunk(x, pool_hist, conv_hist, pos0, seqs, steps, weights, g_final, u_tiles, v_tiles):
    nseq, length, _ = x.shape
    n_tok = nseq * length
    x1, h2, meta_t, pgate_t, new_pool, new_conv = _mixer_router(
        x, pool_hist, conv_hist, pos0, seqs, steps, weights)
    peer_out = _peer_experts(meta_t.T.reshape(-1), pgate_t.T.reshape(-1),
                             h2.reshape(n_tok * ROW_TILE, V7X_LANES), u_tiles, v_tiles)
    y = _final_norm(x1.reshape(n_tok, D_MODEL), peer_out.reshape(n_tok, D_MODEL), g_final)
    return y.reshape(nseq, length, D_MODEL), new_pool[None], new_conv[None]


def kernel(x_prompt, x_sample, state_pool, state_conv, g_mix, w_in, b_in, w_pool, b_pool,
           pool_scale, w_dw, b_dw, ln_g, ln_b, w_out, g_ffn, w_query, sub_keys, peer_u,
           peer_v, g_final):
    assert g_mix.shape[0] == 1, "single-layer trunk"
    wq_hi, wq_lo = _split_bf16(w_query[0])
    keys = sub_keys[0].reshape(2 * PEER_HEADS, PEER_NK, PEER_DKH)
    keys_hi, keys_lo = _split_bf16(keys)
    row = lambda v: v.reshape(1, -1)
    weights = (
        row(g_mix[0]), w_in[0].astype(BF16), row(b_in[0]), w_pool[0].astype(BF16),
        row(b_pool[0]), row(pool_scale[0]), w_dw[0], row(b_dw[0]), row(ln_g[0]),
        row(ln_b[0]), w_out[0].astype(BF16), row(g_ffn[0]), wq_hi, wq_lo, keys_hi, keys_lo)
    g_final_row = row(g_final)
    u_tiles = peer_u[0].reshape(PEER_N * ROW_TILE, V7X_LANES)
    v_tiles = peer_v[0].reshape(PEER_N * ROW_TILE, V7X_LANES)

    n_prompt = x_prompt.shape[0]
    zero_pool = jnp.zeros((n_prompt, POOL_HIST, POOL_W), F32)
    zero_conv = jnp.zeros((n_prompt, CONV_HIST, CONV_W), F32)
    y_p, pool_p, conv_p = _trunk(
        x_prompt, zero_pool, zero_conv, 0, 1, BLOCK_TOKENS, weights, g_final_row,
        u_tiles, v_tiles)
    dec_seq = x_sample.shape[1]
    y_s, pool_s, conv_s = _trunk(
        x_sample, state_pool[0], state_conv[0], PAST_LEN, BLOCK_TOKENS // dec_seq, dec_seq,
        weights, g_final_row, u_tiles, v_tiles)
    return (y_p, y_s, pool_p, conv_p, pool_s, conv_s)
```

```python
import functools
import math

import jax
import jax.numpy as jnp
from jax import lax
from jax.experimental import pallas as pl
from jax.experimental.pallas import tpu as pltpu

D_MODEL = 1024
POOL_W = 512
POOL_WINDOWS = (2, 4, 8, 16)
POOL_GW = POOL_W // len(POOL_WINDOWS)
POOL_HIST = max(POOL_WINDOWS) - 1
CONV_W = 512
CONV_K = 31
CONV_HIST = CONV_K - 1
IN_W = POOL_W + 2 * CONV_W
PEER_HEADS = 8
PEER_NK = 128
PEER_N = PEER_NK * PEER_NK
PEER_DKH = 128
PEER_TOPK = 16
PEER_SLOTS = PEER_HEADS * PEER_TOPK
PAST_LEN = 1024
EPS = 1e-6

V7X_LANES = 128
V7X_SUBLANES = 8
V7X_VMEM_BYTES = 64 * 1024 * 1024

POOL_HALO = 16
CONV_HALO = 32

BLOCK_TOKENS = 256
PEER_BLOCK = 128
NORM_BLOCK = 512

ROW_TILE = D_MODEL // V7X_LANES
PEER_CHUNKS = 4
CHUNK_EXPERTS = PEER_N // PEER_CHUNKS
GROUP = 2 * V7X_SUBLANES
LIST_ROWS = 192
META_ROWS = 256
GSTART_ROW = LIST_ROWS
SLAB_ROWS = 512
GROUPS_PER_STEP = 4
ZERO_ENTRY = GSTART_ROW + GROUP
assert ZERO_ENTRY >= GSTART_ROW + V7X_SUBLANES and ZERO_ENTRY + GROUP <= META_ROWS
assert META_ROWS & (META_ROWS - 1) == 0
assert LIST_ROWS >= PEER_SLOTS + PEER_CHUNKS * (GROUP - 1)
assert ROW_TILE == V7X_SUBLANES

F32 = jnp.float32
BF16 = jnp.bfloat16


def _rms(x, g):
    return x * lax.rsqrt(jnp.mean(x * x, axis=-1, keepdims=True) + EPS) * g


def _sigmoid(x):
    return 1.0 / (1.0 + jnp.exp(-x))


def _gelu_tanh(x):
    c = math.sqrt(2.0 / math.pi)
    return 0.5 * x * (1.0 + jnp.tanh(c * (x + 0.044715 * (x * x * x))))


def _split_bf16(a):
    hi = a.astype(BF16)
    lo = (a - hi.astype(F32)).astype(BF16)
    return hi, lo


def _dot(a, b):
    return jnp.dot(a, b, preferred_element_type=F32)


def _dot_nt(a, b):
    return lax.dot_general(a, b, (((1,), (1,)), ((), ())), preferred_element_type=F32)


def _row_iota(shape):
    return lax.broadcasted_iota(jnp.int32, shape, 0).astype(F32)


def _topk_rows(s, k):
    n, width = s.shape
    rows = _row_iota((n, width))
    out_rows = lax.broadcasted_iota(jnp.int32, (k, width), 0)
    vals = jnp.zeros((k, width), F32)
    idxs = jnp.zeros((k, width), F32)
    for j in range(k):
        m = jnp.max(s, axis=0, keepdims=True)
        i = jnp.min(jnp.where(s == m, rows, float(n)), axis=0, keepdims=True)
        vals = jnp.where(out_rows == j, m, vals)
        idxs = jnp.where(out_rows == j, i, idxs)
        s = jnp.where(rows == i, -jnp.inf, s)
    return vals, idxs


def _select_row(table, row):
    return jnp.sum(jnp.where(_row_iota(table.shape) == row, table, 0.0), axis=0, keepdims=True)


def _mixer_router_kernel(
        x_ref, ph_ref, ch_ref, g_mix_ref, w_in_ref, b_in_ref, w_pool_ref, b_pool_ref,
        pool_scale_ref, w_dw_ref, b_dw_ref, ln_g_ref, ln_b_ref, w_out_ref, g_ffn_ref,
        wq_hi_ref, wq_lo_ref, keys_hi_ref, keys_lo_ref,
        x1_ref, h2_ref, meta_ref, pgate_ref, new_pool_ref, new_conv_ref,
        zbuf, ubuf, s_scr, top_s, top_i, e_scr, g_scr, key_scr, p_scr, off_scr,
        *, pos0, seqs, steps):
    tb = pl.program_id(1)
    tokens = seqs * steps

    @pl.when(tb == 0)
    def _():
        zbuf[:, POOL_HALO - POOL_HIST:POOL_HALO, :] = ph_ref[...]
        ubuf[:, CONV_HALO - CONV_HIST:CONV_HALO, :] = ch_ref[...]

    x = x_ref[...].reshape(tokens, D_MODEL)
    h = _rms(x, g_mix_ref[...])
    z = _dot(h.astype(BF16), w_in_ref[...]) + b_in_ref[...]
    za = z[:, :POOL_W]
    zv = z[:, POOL_W:POOL_W + CONV_W]
    zg = z[:, POOL_W + CONV_W:]
    u = zv * _sigmoid(zg)
    zbuf[:, POOL_HALO:, :] = za.reshape(seqs, steps, POOL_W)
    ubuf[:, CONV_HALO:, :] = u.reshape(seqs, steps, CONV_W)

    pos = pos0 + tb * steps + lax.broadcasted_iota(jnp.int32, (steps, 1), 0)

    a_parts, c_parts = [], []
    for s in range(seqs):
        groups = []
        for g, w in enumerate(POOL_WINDOWS):
            cols = slice(g * POOL_GW, (g + 1) * POOL_GW)
            tot = zbuf[s, POOL_HALO:POOL_HALO + steps, cols]
            for j in range(1, w):
                tot = tot + zbuf[s, POOL_HALO - j:POOL_HALO - j + steps, cols]
            cnt = jnp.minimum(pos + 1, w).astype(F32)
            d = tot / cnt - zbuf[s, POOL_HALO:POOL_HALO + steps, cols]
            groups.append(_dot(d.astype(BF16), w_pool_ref[g]))
        a_parts.append(jnp.concatenate(groups, axis=-1))

        conv = jnp.zeros((steps, CONV_W), F32)
        for k in range(CONV_K):
            lo = CONV_HALO - CONV_HIST + k
            conv = conv + ubuf[s, lo:lo + steps, :] * w_dw_ref[k:k + 1, :]
        c_parts.append(conv)

    a = jnp.concatenate(a_parts, axis=0) if seqs > 1 else a_parts[0]
    a = (a + b_pool_ref[...]) * pool_scale_ref[...]
    c = jnp.concatenate(c_parts, axis=0) if seqs > 1 else c_parts[0]
    c = c + b_dw_ref[...]
    mu = jnp.mean(c, axis=-1, keepdims=True)
    var = jnp.mean(jnp.square(c - mu), axis=-1, keepdims=True)
    c = (c - mu) * lax.rsqrt(var + EPS) * ln_g_ref[...] + ln_b_ref[...]
    c = c * _sigmoid(c)

    x1 = (x + _dot(a.astype(BF16), w_out_ref[:POOL_W, :])
          + _dot(c.astype(BF16), w_out_ref[POOL_W:, :]))
    x1_ref[...] = x1.reshape(seqs, steps, D_MODEL)

    new_pool_ref[...] = zbuf[:, steps + POOL_HALO - POOL_HIST:steps + POOL_HALO, :]
    new_conv_ref[...] = ubuf[:, steps + CONV_HALO - CONV_HIST:steps + CONV_HALO, :]
    zbuf[:, POOL_HALO - POOL_HIST:POOL_HALO, :] = (
        zbuf[:, steps + POOL_HALO - POOL_HIST:steps + POOL_HALO, :])
    ubuf[:, CONV_HALO - CONV_HIST:CONV_HALO, :] = (
        ubuf[:, steps + CONV_HALO - CONV_HIST:steps + CONV_HALO, :])

    h2 = _rms(x1, g_ffn_ref[...])
    h2_ref[...] = h2
    h2_hi, h2_lo = _split_bf16(h2)
    q = (_dot(h2_hi, wq_hi_ref[...]) + _dot(h2_lo, wq_hi_ref[...])
         + _dot(h2_hi, wq_lo_ref[...]))
    for l in range(2 * PEER_HEADS):
        q_hi, q_lo = _split_bf16(q[:, l * PEER_DKH:(l + 1) * PEER_DKH])
        k_hi = keys_hi_ref[l]
        k_lo = keys_lo_ref[l]
        s_scr[l] = _dot_nt(k_hi, q_hi) + _dot_nt(k_hi, q_lo) + _dot_nt(k_lo, q_hi)

    lane_groups = tokens // V7X_LANES

    def lanes_of(it):
        return pl.ds(pl.multiple_of((it % lane_groups) * V7X_LANES, V7X_LANES), V7X_LANES)

    def stage1(l, carry):
        for lg in range(lane_groups):
            lanes = slice(lg * V7X_LANES, (lg + 1) * V7X_LANES)
            vals, idxs = _topk_rows(s_scr[l, :, lanes], PEER_TOPK)
            top_s[l, :, lanes] = vals
            top_i[l, :, lanes] = idxs
        return carry

    lax.fori_loop(0, 2 * PEER_HEADS, stage1, 0)

    def stage2(it, carry):
        hd = it // lane_groups
        lanes = lanes_of(it)
        s1 = top_s[2 * hd, :, lanes]
        s2 = top_s[2 * hd + 1, :, lanes]
        i1 = top_i[2 * hd, :, lanes]
        i2 = top_i[2 * hd + 1, :, lanes]
        comb = jnp.concatenate([s1[r:r + 1, :] + s2 for r in range(PEER_TOPK)], axis=0)
        f_s, f_i = _topk_rows(comb, PEER_TOPK)
        out_rows = lax.broadcasted_iota(jnp.int32, (PEER_TOPK, V7X_LANES), 0)
        eidx = jnp.zeros((PEER_TOPK, V7X_LANES), F32)
        for j in range(PEER_TOPK):
            fj = f_i[j:j + 1, :]
            ra = jnp.floor(fj * (1.0 / PEER_TOPK))
            rb = fj - ra * PEER_TOPK
            e = _select_row(i1, ra) * PEER_NK + _select_row(i2, rb)
            eidx = jnp.where(out_rows == j, e, eidx)
        p = jnp.exp(f_s - jnp.max(f_s, axis=0, keepdims=True))
        gate = p / jnp.sum(p, axis=0, keepdims=True)
        rows = pl.ds(pl.multiple_of(hd * PEER_TOPK, PEER_TOPK), PEER_TOPK)
        e_scr[rows, lanes] = eidx
        g_scr[rows, lanes] = gate
        return carry

    lax.fori_loop(0, PEER_HEADS * lane_groups, stage2, 0)

    def stage3(lg):
        lanes = slice(lg * V7X_LANES, (lg + 1) * V7X_LANES)
        e = e_scr[:, lanes]
        slot = _row_iota((PEER_SLOTS, V7X_LANES))
        key = e * PEER_SLOTS + slot
        key_scr[:, lanes] = key

        def tile_of(ref, kb):
            return ref[pl.ds(pl.multiple_of(kb * V7X_SUBLANES, V7X_SUBLANES), V7X_SUBLANES), lanes]

        def rank_body(kb, rank):
            keys8 = tile_of(key_scr, kb)
            for j in range(V7X_SUBLANES):
                rank = rank + jnp.where(keys8[j:j + 1, :] < key, 1.0, 0.0)
            return rank

        rank = lax.fori_loop(0, PEER_SLOTS // V7X_SUBLANES, rank_body, jnp.zeros_like(key))
        chunk = jnp.floor(e * (1.0 / CHUNK_EXPERTS))
        start = jnp.zeros((1, V7X_LANES), F32)
        pstart = jnp.zeros((1, V7X_LANES), F32)
        shift = jnp.zeros_like(key)
        grow = lax.broadcasted_iota(jnp.int32, (V7X_SUBLANES, V7X_LANES), 0)
        gstart = jnp.zeros((V7X_SUBLANES, V7X_LANES), F32)
        for ch in range(PEER_CHUNKS):
            mine = chunk == float(ch)
            n = jnp.sum(jnp.where(mine, 1.0, 0.0), axis=0, keepdims=True)
            shift = jnp.where(mine, pstart - start, shift)
            start = start + n
            pstart = pstart + jnp.ceil(n * (1.0 / GROUP)) * GROUP
            gstart = jnp.where(grow == ch + 1, pstart * (1.0 / GROUP), gstart)
        p_scr[:, lanes] = rank + shift
        off_scr[:, lanes] = (e - chunk * CHUNK_EXPERTS) * ROW_TILE

        list_rows = _row_iota((LIST_ROWS, V7X_LANES))

        def place(kb, lists):
            offs, gates = lists
            p8, off8, gate8 = tile_of(p_scr, kb), tile_of(off_scr, kb), tile_of(g_scr, kb)
            for j in range(V7X_SUBLANES):
                hit = list_rows == p8[j:j + 1, :]
                offs = jnp.where(hit, off8[j:j + 1, :], offs)
                gates = jnp.where(hit, gate8[j:j + 1, :], gates)
            return offs, gates

        zeros = jnp.zeros((LIST_ROWS, V7X_LANES), F32)
        offs, gates = lax.fori_loop(0, PEER_SLOTS // V7X_SUBLANES, place, (zeros, zeros))
        meta_ref[0:LIST_ROWS, lanes] = offs.astype(jnp.int32)
        meta_ref[GSTART_ROW:GSTART_ROW + V7X_SUBLANES, lanes] = gstart.astype(jnp.int32)
        meta_ref[GSTART_ROW + V7X_SUBLANES:, lanes] = jnp.zeros(
            (META_ROWS - GSTART_ROW - V7X_SUBLANES, V7X_LANES), jnp.int32)
        pgate_ref[0:LIST_ROWS, lanes] = gates
        pgate_ref[LIST_ROWS:, lanes] = jnp.zeros((META_ROWS - LIST_ROWS, V7X_LANES), F32)

    for lg in range(lane_groups):
        stage3(lg)


def _const_spec(shape):
    zeros = (0,) * len(shape)
    return pl.BlockSpec(shape, lambda i, j: zeros, pipeline_mode=pl.Buffered(1))


def _mixer_router(x, pool_hist, conv_hist, pos0, seqs, steps, weights):
    nseq, length, _ = x.shape
    tokens = seqs * steps
    grid = (nseq // seqs, length // steps)
    blocks_per_seq_group = length // steps
    per_time = pl.BlockSpec((seqs, steps, D_MODEL), lambda i, j: (i, j, 0))
    per_token = pl.BlockSpec((tokens, D_MODEL), lambda i, j: (i * blocks_per_seq_group + j, 0))
    per_seq_pool = pl.BlockSpec((seqs, POOL_HIST, POOL_W), lambda i, j: (i, 0, 0))
    per_seq_conv = pl.BlockSpec((seqs, CONV_HIST, CONV_W), lambda i, j: (i, 0, 0))

    def slot_spec(rows):
        return pl.BlockSpec((rows, tokens), lambda i, j: (0, i * blocks_per_seq_group + j))

    n_tok = nseq * length
    kernel = functools.partial(_mixer_router_kernel, pos0=pos0, seqs=seqs, steps=steps)
    slot_scratch = pltpu.VMEM((PEER_SLOTS, tokens), F32)
    return pl.pallas_call(
        kernel,
        grid=grid,
        in_specs=[per_time, per_seq_pool, per_seq_conv] + [_const_spec(w.shape) for w in weights],
        out_specs=[per_time, per_token, slot_spec(META_ROWS), slot_spec(META_ROWS),
                   per_seq_pool, per_seq_conv],
        out_shape=[
            jax.ShapeDtypeStruct((nseq, length, D_MODEL), F32),
            jax.ShapeDtypeStruct((n_tok, D_MODEL), F32),
            jax.ShapeDtypeStruct((META_ROWS, n_tok), jnp.int32),
            jax.ShapeDtypeStruct((META_ROWS, n_tok), F32),
            jax.ShapeDtypeStruct((nseq, POOL_HIST, POOL_W), F32),
            jax.ShapeDtypeStruct((nseq, CONV_HIST, CONV_W), F32),
        ],
        scratch_shapes=[
            pltpu.VMEM((seqs, POOL_HALO + steps, POOL_W), F32),
            pltpu.VMEM((seqs, CONV_HALO + steps, CONV_W), F32),
            pltpu.VMEM((2 * PEER_HEADS, PEER_NK, tokens), F32),
            pltpu.VMEM((2 * PEER_HEADS, PEER_TOPK, tokens), F32),
            pltpu.VMEM((2 * PEER_HEADS, PEER_TOPK, tokens), F32),
            slot_scratch, slot_scratch, slot_scratch, slot_scratch, slot_scratch,
        ],
        compiler_params=pltpu.CompilerParams(
            dimension_semantics=("arbitrary", "arbitrary"),
            vmem_limit_bytes=V7X_VMEM_BYTES * 3 // 4),
        name="mixer_router",
    )(x, pool_hist, conv_hist, *weights)


def _tile_rows(index):
    return pl.ds(pl.multiple_of(index * ROW_TILE, ROW_TILE), ROW_TILE)


def _sum_tiles(p):
    sub = lax.broadcasted_iota(jnp.int32, (V7X_SUBLANES, V7X_LANES), 0)
    m4 = sub < 4
    m2 = (sub & 2) == 0
    m1 = (sub & 1) == 0
    b = [jnp.where(m4, p[i], p[i + 4]) + pltpu.roll(jnp.where(m4, p[i + 4], p[i]), 4, 0)
         for i in range(4)]
    c = [jnp.where(m2, b[i], pltpu.roll(b[i + 2], 2, 0))
         + jnp.where(m2, pltpu.roll(b[i], 6, 0), b[i + 2]) for i in range(2)]
    return (jnp.where(m1, c[0], pltpu.roll(c[1], 1, 0))
            + jnp.where(m1, pltpu.roll(c[0], 7, 0), c[1]))


def _tree_sum(terms):
    while len(terms) > 1:
        terms = [terms[i] + terms[i + 1] for i in range(0, len(terms), 2)]
    return terms[0]


def _peer_experts_kernel(meta_ref, pgate_ref, h2_ref, acc_ref, u_ref, v_ref, out_ref,
                         pair_scr, group_tab):
    chunk = pl.program_id(0)

    @pl.when((chunk == 0) & (pl.program_id(1) == 0))
    def _():
        pair_scr[...] = jnp.zeros(pair_scr.shape, F32)

    def list_token(t, count):
        bounds = t * META_ROWS + GSTART_ROW + chunk

        def list_group(g, count):
            group_tab[count] = t * META_ROWS + g * GROUP
            return count + 1

        return lax.fori_loop(meta_ref[bounds], meta_ref[bounds + 1], list_group, count)

    n_groups = lax.fori_loop(0, PEER_BLOCK, list_token, 0)
    for q in range(GROUPS_PER_STEP - 1):
        group_tab[n_groups + q] = ZERO_ENTRY
    n_steps = (n_groups + GROUPS_PER_STEP - 1) // GROUPS_PER_STEP

    def token_of(entry):
        return lax.shift_right_logical(entry, META_ROWS.bit_length() - 1)

    def dots(i, carry):
        for q in range(GROUPS_PER_STEP):
            slot = i * GROUPS_PER_STEP + q
            entry = group_tab[slot]
            h = h2_ref[_tile_rows(token_of(entry)), :]
            for half in range(GROUP // V7X_SUBLANES):
                prods = []
                for j in range(V7X_SUBLANES):
                    off = meta_ref[entry + half * V7X_SUBLANES + j]
                    prods.append(u_ref[pl.ds(pl.multiple_of(off, ROW_TILE), ROW_TILE), :] * h)
                pair_scr[_tile_rows(slot * 2 + half), :] = _sum_tiles(prods)
        return carry

    lax.fori_loop(0, n_steps, dots, 0)

    def activate(i, carry):
        rows = pl.ds(pl.multiple_of(i * SLAB_ROWS, SLAB_ROWS), SLAB_ROWS)
        act = jnp.sum(pair_scr[rows, :], axis=-1, keepdims=True)
        pair_scr[rows, :] = jnp.broadcast_to(_gelu_tanh(act), (SLAB_ROWS, V7X_LANES))
        return carry

    n_rows = n_steps * (GROUPS_PER_STEP * GROUP)
    lax.fori_loop(0, (n_rows + SLAB_ROWS - 1) // SLAB_ROWS, activate, 0)

    out_ref[...] = acc_ref[...]

    def mix(i, carry):
        for q in range(GROUPS_PER_STEP):
            slot = i * GROUPS_PER_STEP + q
            entry = group_tab[slot]
            terms = []
            for half in range(GROUP // V7X_SUBLANES):
                coef = pair_scr[_tile_rows(slot * 2 + half), :]
                for j in range(V7X_SUBLANES):
                    k = half * V7X_SUBLANES + j
                    off = meta_ref[entry + k]
                    cj = jnp.broadcast_to(coef[j:j + 1, :], (V7X_SUBLANES, V7X_LANES))
                    v_row = v_ref[pl.ds(pl.multiple_of(off, ROW_TILE), ROW_TILE), :]
                    terms.append((cj * pgate_ref[entry + k]) * v_row)
            rows = _tile_rows(token_of(entry))
            out_ref[rows, :] = out_ref[rows, :] + _tree_sum(terms)
        return carry

    lax.fori_loop(0, n_steps, mix, 0)


def _peer_experts(meta, pgate, h2_tiles, peer_u, peer_v):
    n_tok = meta.shape[0] // META_ROWS
    tok_spec = pl.BlockSpec((PEER_BLOCK * ROW_TILE, V7X_LANES), lambda c, i: (i, 0))
    table_spec = pl.BlockSpec((CHUNK_EXPERTS * ROW_TILE, V7X_LANES), lambda c, i: (c, 0),
                              pipeline_mode=pl.Buffered(1))
    max_groups = PEER_BLOCK * (PEER_SLOTS // GROUP) + GROUPS_PER_STEP - 1
    pair_rows = -(-max_groups * GROUP // SLAB_ROWS) * SLAB_ROWS
    table_bytes = 2 * CHUNK_EXPERTS * D_MODEL * 4
    block_bytes = PEER_BLOCK * D_MODEL * 4
    vmem_bytes = table_bytes + pair_rows * V7X_LANES * 4 + 6 * block_bytes + (4 << 20)
    assert vmem_bytes <= V7X_VMEM_BYTES
    return pl.pallas_call(
        _peer_experts_kernel,
        grid=(PEER_CHUNKS, n_tok // PEER_BLOCK),
        in_specs=[
            pl.BlockSpec((PEER_BLOCK * META_ROWS,), lambda c, i: (i,), memory_space=pltpu.SMEM),
            pl.BlockSpec((PEER_BLOCK * META_ROWS,), lambda c, i: (i,), memory_space=pltpu.SMEM),
            tok_spec, tok_spec, table_spec, table_spec,
        ],
        out_specs=tok_spec,
        out_shape=jax.ShapeDtypeStruct((n_tok * ROW_TILE, V7X_LANES), F32),
        scratch_shapes=[pltpu.VMEM((pair_rows, V7X_LANES), F32),
                        pltpu.SMEM((max_groups,), jnp.int32)],
        input_output_aliases={3: 0},
        compiler_params=pltpu.CompilerParams(
            dimension_semantics=("arbitrary", "arbitrary"),
            vmem_limit_bytes=vmem_bytes),
        name="peer_experts",
    )(meta, pgate, h2_tiles, jnp.zeros((n_tok * ROW_TILE, V7X_LANES), F32), peer_u, peer_v)


def _final_norm_kernel(x1_ref, peer_ref, g_ref, y_ref):
    y_ref[...] = _rms(x1_ref[...] + peer_ref[...], g_ref[...])


def _final_norm(x1, peer_out, g_final):
    n_tok = x1.shape[0]
    spec = pl.BlockSpec((NORM_BLOCK, D_MODEL), lambda i: (i, 0))
    return pl.pallas_call(
        _final_norm_kernel,
        grid=(n_tok // NORM_BLOCK,),
        in_specs=[spec, spec, pl.BlockSpec((1, D_MODEL), lambda i: (0, 0))],
        out_specs=spec,
        out_shape=jax.ShapeDtypeStruct((n_tok, D_MODEL), F32),
        name="final_norm",
    )(x1, peer_out, g_final)


def _trunk(x, pool_hist, conv_hist, pos0, seqs, steps, weights, g_final, u_tiles, v_tiles):
    nseq, length, _ = x.shape
    n_tok = nseq * length
    x1, h2, meta_t, pgate_t, new_pool, new_conv = _mixer_router(
        x, pool_hist, conv_hist, pos0, seqs, steps, weights)
    peer_out = _peer_experts(meta_t.T.reshape(-1), pgate_t.T.reshape(-1),
                             h2.reshape(n_tok * ROW_TILE, V7X_LANES), u_tiles, v_tiles)
    y = _final_norm(x1.reshape(n_tok, D_MODEL), peer_out.reshape(n_tok, D_MODEL), g_final)
    return y.reshape(nseq, length, D_MODEL), new_pool[None], new_conv[None]


def kernel(x_prompt, x_sample, state_pool, state_conv, g_mix, w_in, b_in, w_pool, b_pool,
           pool_scale, w_dw, b_dw, ln_g, ln_b, w_out, g_ffn, w_query, sub_keys, peer_u,
           peer_v, g_final):
    assert g_mix.shape[0] == 1, "single-layer trunk"
    wq_hi, wq_lo = _split_bf16(w_query[0])
    keys = sub_keys[0].reshape(2 * PEER_HEADS, PEER_NK, PEER_DKH)
    keys_hi, keys_lo = _split_bf16(keys)
    row = lambda v: v.reshape(1, -1)
    weights = (
        row(g_mix[0]), w_in[0].astype(BF16), row(b_in[0]), w_pool[0].astype(BF16),
        row(b_pool[0]), row(pool_scale[0]), w_dw[0], row(b_dw[0]), row(ln_g[0]),
        row(ln_b[0]), w_out[0].astype(BF16), row(g_ffn[0]), wq_hi, wq_lo, keys_hi, keys_lo)
    g_final_row = row(g_final)
    u_tiles = peer_u[0].reshape(PEER_N * ROW_TILE, V7X_LANES)
    v_tiles = peer_v[0].reshape(PEER_N * ROW_TILE, V7X_LANES)

    n_prompt = x_prompt.shape[0]
    zero_pool = jnp.zeros((n_prompt, POOL_HIST, POOL_W), F32)
    zero_conv = jnp.zeros((n_prompt, CONV_HIST, CONV_W), F32)
    y_p, pool_p, conv_p = _trunk(
        x_prompt, zero_pool, zero_conv, 0, 1, BLOCK_TOKENS, weights, g_final_row,
        u_tiles, v_tiles)
    dec_seq = x_sample.shape[1]
    y_s, pool_s, conv_s = _trunk(
        x_sample, state_pool[0], state_conv[0], PAST_LEN, BLOCK_TOKENS // dec_seq, dec_seq,
        weights, g_final_row, u_tiles, v_tiles)
    return (y_p, y_s, pool_p, conv_p, pool_s, conv_s)
```

```python
import functools
import math

import jax
import jax.numpy as jnp
from jax import lax
from jax.experimental import pallas as pl
from jax.experimental.pallas import tpu as pltpu

D_MODEL = 1024
POOL_W = 512
POOL_WINDOWS = (2, 4, 8, 16)
POOL_GW = POOL_W // len(POOL_WINDOWS)
POOL_HIST = max(POOL_WINDOWS) - 1
CONV_W = 512
CONV_K = 31
CONV_HIST = CONV_K - 1
IN_W = POOL_W + 2 * CONV_W
PEER_HEADS = 8
PEER_NK = 128
PEER_N = PEER_NK * PEER_NK
PEER_DKH = 128
PEER_TOPK = 16
PEER_SLOTS = PEER_HEADS * PEER_TOPK
PAST_LEN = 1024
EPS = 1e-6

V7X_LANES = 128
V7X_SUBLANES = 8
V7X_VMEM_BYTES = 64 * 1024 * 1024

POOL_HALO = 16
CONV_HALO = 32

BLOCK_TOKENS = 256
PEER_BLOCK = 128
NORM_BLOCK = 512

ROW_TILE = D_MODEL // V7X_LANES
PEER_CHUNKS = 4
CHUNK_EXPERTS = PEER_N // PEER_CHUNKS
GROUP = 2 * V7X_SUBLANES
LIST_ROWS = 192
META_ROWS = 256
GSTART_ROW = LIST_ROWS
SLAB_ROWS = 512
GROUPS_PER_STEP = 4
MAX_TOKEN_GROUPS = PEER_SLOTS // GROUP
ZERO_ENTRY = GSTART_ROW + GROUP
assert ZERO_ENTRY >= GSTART_ROW + V7X_SUBLANES and ZERO_ENTRY + GROUP <= META_ROWS
assert META_ROWS & (META_ROWS - 1) == 0
assert LIST_ROWS >= PEER_SLOTS + PEER_CHUNKS * (GROUP - 1)
assert ROW_TILE == V7X_SUBLANES

F32 = jnp.float32
BF16 = jnp.bfloat16


def _rms(x, g):
    return x * lax.rsqrt(jnp.mean(x * x, axis=-1, keepdims=True) + EPS) * g


def _sigmoid(x):
    return 1.0 / (1.0 + jnp.exp(-x))


def _gelu_tanh(x):
    c = math.sqrt(2.0 / math.pi)
    return 0.5 * x * (1.0 + jnp.tanh(c * (x + 0.044715 * (x * x * x))))


def _split_bf16(a):
    hi = a.astype(BF16)
    lo = (a - hi.astype(F32)).astype(BF16)
    return hi, lo


def _dot(a, b):
    return jnp.dot(a, b, preferred_element_type=F32)


def _dot_nt(a, b):
    return lax.dot_general(a, b, (((1,), (1,)), ((), ())), preferred_element_type=F32)


def _row_iota(shape):
    return lax.broadcasted_iota(jnp.int32, shape, 0).astype(F32)


def _topk_rows(s, k, ids=None):
    n, width = s.shape
    if ids is None:
        ids = _row_iota((n, width))
    out_rows = lax.broadcasted_iota(jnp.int32, (k, width), 0)
    vals = jnp.zeros((k, width), F32)
    idxs = jnp.zeros((k, width), F32)
    for j in range(k):
        m = jnp.max(s, axis=0, keepdims=True)
        i = jnp.min(jnp.where(s == m, ids, jnp.inf), axis=0, keepdims=True)
        vals = jnp.where(out_rows == j, m, vals)
        idxs = jnp.where(out_rows == j, i, idxs)
        s = jnp.where(ids == i, -jnp.inf, s)
    return vals, idxs


def _lookup_rows(table, rows):
    out = jnp.zeros(rows.shape, F32)
    for r in range(table.shape[0]):
        out = jnp.where(rows == float(r), table[r:r + 1, :], out)
    return out


def _pair_candidates(s1, s2):
    width = s1.shape[1]
    sub = lax.broadcasted_iota(jnp.int32, (V7X_SUBLANES, width), 0)
    subf = sub.astype(F32)
    low = sub < 4
    s2_lo, s2_hi = s2[0:8, :], s2[8:16, :]
    s2_lo_twice = jnp.where(low, s2_lo, pltpu.roll(s2_lo, 4, 0))
    b_twice = jnp.where(low, subf, subf - 4.0)
    sums = [s1[0:1, :] + s2_lo, s1[0:1, :] + s2_hi]
    ids = [subf, subf + 8.0]
    for a in (1, 2, 3):
        sums.append(s1[a:a + 1, :] + s2_lo)
        ids.append(subf + 16.0 * a)
    for a in (4, 6):
        sums.append(jnp.where(low, s1[a:a + 1, :], s1[a + 1:a + 2, :]) + s2_lo_twice)
        ids.append(jnp.where(low, 16.0 * a, 16.0 * (a + 1)) + b_twice)
    sums.append(s1[8:16, :] + s2[0:1, :])
    ids.append((subf + 8.0) * 16.0)
    return jnp.concatenate(sums, axis=0), jnp.concatenate(ids, axis=0)


def _mixer_router_kernel(
        x_ref, ph_ref, ch_ref, g_mix_ref, w_in_ref, b_in_ref, w_pool_ref, b_pool_ref,
        pool_scale_ref, w_dw_ref, b_dw_ref, ln_g_ref, ln_b_ref, w_out_ref, g_ffn_ref,
        wq_hi_ref, wq_lo_ref, keys_hi_ref, keys_lo_ref,
        x1_ref, h2_ref, meta_ref, pgate_ref, new_pool_ref, new_conv_ref,
        zbuf, ubuf, s_scr, top_s, top_i, e_scr, g_scr, key_scr, p_scr, off_scr,
        *, pos0, seqs, steps):
    tb = pl.program_id(1)
    tokens = seqs * steps

    @pl.when(tb == 0)
    def _():
        zbuf[:, POOL_HALO - POOL_HIST:POOL_HALO, :] = ph_ref[...]
        ubuf[:, CONV_HALO - CONV_HIST:CONV_HALO, :] = ch_ref[...]

    x = x_ref[...].reshape(tokens, D_MODEL)
    h = _rms(x, g_mix_ref[...])
    z = _dot(h.astype(BF16), w_in_ref[...]) + b_in_ref[...]
    za = z[:, :POOL_W]
    zv = z[:, POOL_W:POOL_W + CONV_W]
    zg = z[:, POOL_W + CONV_W:]
    u = zv * _sigmoid(zg)
    zbuf[:, POOL_HALO:, :] = za.reshape(seqs, steps, POOL_W)
    ubuf[:, CONV_HALO:, :] = u.reshape(seqs, steps, CONV_W)

    pos = pos0 + tb * steps + lax.broadcasted_iota(jnp.int32, (steps, 1), 0)

    a_parts, c_parts = [], []
    for s in range(seqs):
        groups = []
        for g, w in enumerate(POOL_WINDOWS):
            cols = slice(g * POOL_GW, (g + 1) * POOL_GW)
            tot = zbuf[s, POOL_HALO:POOL_HALO + steps, cols]
            for j in range(1, w):
                tot = tot + zbuf[s, POOL_HALO - j:POOL_HALO - j + steps, cols]
            cnt = jnp.minimum(pos + 1, w).astype(F32)
            d = tot / cnt - zbuf[s, POOL_HALO:POOL_HALO + steps, cols]
            groups.append(_dot(d.astype(BF16), w_pool_ref[g]))
        a_parts.append(jnp.concatenate(groups, axis=-1))

        conv = jnp.zeros((steps, CONV_W), F32)
        for k in range(CONV_K):
            lo = CONV_HALO - CONV_HIST + k
            conv = conv + ubuf[s, lo:lo + steps, :] * w_dw_ref[k:k + 1, :]
        c_parts.append(conv)

    a = jnp.concatenate(a_parts, axis=0) if seqs > 1 else a_parts[0]
    a = (a + b_pool_ref[...]) * pool_scale_ref[...]
    c = jnp.concatenate(c_parts, axis=0) if seqs > 1 else c_parts[0]
    c = c + b_dw_ref[...]
    mu = jnp.mean(c, axis=-1, keepdims=True)
    var = jnp.mean(jnp.square(c - mu), axis=-1, keepdims=True)
    c = (c - mu) * lax.rsqrt(var + EPS) * ln_g_ref[...] + ln_b_ref[...]
    c = c * _sigmoid(c)

    x1 = (x + _dot(a.astype(BF16), w_out_ref[:POOL_W, :])
          + _dot(c.astype(BF16), w_out_ref[POOL_W:, :]))
    x1_ref[...] = x1.reshape(seqs, steps, D_MODEL)

    new_pool_ref[...] = zbuf[:, steps + POOL_HALO - POOL_HIST:steps + POOL_HALO, :]
    new_conv_ref[...] = ubuf[:, steps + CONV_HALO - CONV_HIST:steps + CONV_HALO, :]
    zbuf[:, POOL_HALO - POOL_HIST:POOL_HALO, :] = (
        zbuf[:, steps + POOL_HALO - POOL_HIST:steps + POOL_HALO, :])
    ubuf[:, CONV_HALO - CONV_HIST:CONV_HALO, :] = (
        ubuf[:, steps + CONV_HALO - CONV_HIST:steps + CONV_HALO, :])

    h2 = _rms(x1, g_ffn_ref[...])
    h2_ref[...] = h2
    h2_hi, h2_lo = _split_bf16(h2)
    q = (_dot(h2_hi, wq_hi_ref[...]) + _dot(h2_lo, wq_hi_ref[...])
         + _dot(h2_hi, wq_lo_ref[...]))
    for l in range(2 * PEER_HEADS):
        q_hi, q_lo = _split_bf16(q[:, l * PEER_DKH:(l + 1) * PEER_DKH])
        k_hi = keys_hi_ref[l]
        k_lo = keys_lo_ref[l]
        s_scr[l] = _dot_nt(k_hi, q_hi) + _dot_nt(k_hi, q_lo) + _dot_nt(k_lo, q_hi)

    lane_groups = tokens // V7X_LANES

    def stage1(l, carry):
        for lg in range(lane_groups):
            lanes = slice(lg * V7X_LANES, (lg + 1) * V7X_LANES)
            vals, idxs = _topk_rows(s_scr[l, :, lanes], PEER_TOPK)
            top_s[l, :, lanes] = vals
            top_i[l, :, lanes] = idxs
        return carry

    lax.fori_loop(0, 2 * PEER_HEADS, stage1, 0)

    def stage2(hd, carry):
        rows = pl.ds(pl.multiple_of(hd * PEER_TOPK, PEER_TOPK), PEER_TOPK)
        for lg in range(lane_groups):
            lanes = slice(lg * V7X_LANES, (lg + 1) * V7X_LANES)
            comb, flat = _pair_candidates(top_s[2 * hd, :, lanes], top_s[2 * hd + 1, :, lanes])
            f_s, f_i = _topk_rows(comb, PEER_TOPK, flat)
            ra = jnp.floor(f_i * (1.0 / PEER_TOPK))
            rb = f_i - ra * PEER_TOPK
            eidx = (_lookup_rows(top_i[2 * hd, :, lanes], ra) * PEER_NK
                    + _lookup_rows(top_i[2 * hd + 1, :, lanes], rb))
            p = jnp.exp(f_s - jnp.max(f_s, axis=0, keepdims=True))
            e_scr[rows, lanes] = eidx
            g_scr[rows, lanes] = p / jnp.sum(p, axis=0, keepdims=True)
        return carry

    lax.fori_loop(0, PEER_HEADS, stage2, 0)

    def stage3(lg):
        lanes = slice(lg * V7X_LANES, (lg + 1) * V7X_LANES)
        e = e_scr[:, lanes]
        slot = _row_iota((PEER_SLOTS, V7X_LANES))
        key = e * PEER_SLOTS + slot
        key_scr[:, lanes] = key

        def tile_of(ref, kb):
            return ref[pl.ds(pl.multiple_of(kb * V7X_SUBLANES, V7X_SUBLANES), V7X_SUBLANES), lanes]

        def rank_body(kb, rank):
            keys8 = tile_of(key_scr, kb)
            for j in range(V7X_SUBLANES):
                rank = rank + jnp.where(keys8[j:j + 1, :] < key, 1.0, 0.0)
            return rank

        rank = lax.fori_loop(0, PEER_SLOTS // V7X_SUBLANES, rank_body, jnp.zeros_like(key))
        chunk = jnp.floor(e * (1.0 / CHUNK_EXPERTS))
        start = jnp.zeros((1, V7X_LANES), F32)
        pstart = jnp.zeros((1, V7X_LANES), F32)
        shift = jnp.zeros_like(key)
        grow = lax.broadcasted_iota(jnp.int32, (V7X_SUBLANES, V7X_LANES), 0)
        gstart = jnp.zeros((V7X_SUBLANES, V7X_LANES), F32)
        for ch in range(PEER_CHUNKS):
            mine = chunk == float(ch)
            n = jnp.sum(jnp.where(mine, 1.0, 0.0), axis=0, keepdims=True)
            shift = jnp.where(mine, pstart - start, shift)
            start = start + n
            pstart = pstart + jnp.ceil(n * (1.0 / GROUP)) * GROUP
            gstart = jnp.where(grow == ch + 1, pstart * (1.0 / GROUP), gstart)
        p_scr[:, lanes] = rank + shift
        off_scr[:, lanes] = (e - chunk * CHUNK_EXPERTS) * ROW_TILE

        list_rows = _row_iota((LIST_ROWS, V7X_LANES))

        def place(kb, lists):
            offs, gates = lists
            p8, off8, gate8 = tile_of(p_scr, kb), tile_of(off_scr, kb), tile_of(g_scr, kb)
            for j in range(V7X_SUBLANES):
                hit = list_rows == p8[j:j + 1, :]
                offs = jnp.where(hit, off8[j:j + 1, :], offs)
                gates = jnp.where(hit, gate8[j:j + 1, :], gates)
            return offs, gates

        zeros = jnp.zeros((LIST_ROWS, V7X_LANES), F32)
        offs, gates = lax.fori_loop(0, PEER_SLOTS // V7X_SUBLANES, place, (zeros, zeros))
        meta_ref[0:LIST_ROWS, lanes] = offs.astype(jnp.int32)
        meta_ref[GSTART_ROW:GSTART_ROW + V7X_SUBLANES, lanes] = gstart.astype(jnp.int32)
        meta_ref[GSTART_ROW + V7X_SUBLANES:, lanes] = jnp.zeros(
            (META_ROWS - GSTART_ROW - V7X_SUBLANES, V7X_LANES), jnp.int32)
        pgate_ref[0:LIST_ROWS, lanes] = gates
        pgate_ref[LIST_ROWS:, lanes] = jnp.zeros((META_ROWS - LIST_ROWS, V7X_LANES), F32)

    for lg in range(lane_groups):
        stage3(lg)


def _const_spec(shape):
    zeros = (0,) * len(shape)
    return pl.BlockSpec(shape, lambda i, j: zeros, pipeline_mode=pl.Buffered(1))


def _mixer_router(x, pool_hist, conv_hist, pos0, seqs, steps, weights):
    nseq, length, _ = x.shape
    tokens = seqs * steps
    grid = (nseq // seqs, length // steps)
    blocks_per_seq_group = length // steps
    per_time = pl.BlockSpec((seqs, steps, D_MODEL), lambda i, j: (i, j, 0))
    per_token = pl.BlockSpec((tokens, D_MODEL), lambda i, j: (i * blocks_per_seq_group + j, 0))
    per_seq_pool = pl.BlockSpec((seqs, POOL_HIST, POOL_W), lambda i, j: (i, 0, 0))
    per_seq_conv = pl.BlockSpec((seqs, CONV_HIST, CONV_W), lambda i, j: (i, 0, 0))

    def slot_spec(rows):
        return pl.BlockSpec((rows, tokens), lambda i, j: (0, i * blocks_per_seq_group + j))

    n_tok = nseq * length
    kernel = functools.partial(_mixer_router_kernel, pos0=pos0, seqs=seqs, steps=steps)
    slot_scratch = pltpu.VMEM((PEER_SLOTS, tokens), F32)
    return pl.pallas_call(
        kernel,
        grid=grid,
        in_specs=[per_time, per_seq_pool, per_seq_conv] + [_const_spec(w.shape) for w in weights],
        out_specs=[per_time, per_token, slot_spec(META_ROWS), slot_spec(META_ROWS),
                   per_seq_pool, per_seq_conv],
        out_shape=[
            jax.ShapeDtypeStruct((nseq, length, D_MODEL), F32),
            jax.ShapeDtypeStruct((n_tok, D_MODEL), F32),
            jax.ShapeDtypeStruct((META_ROWS, n_tok), jnp.int32),
            jax.ShapeDtypeStruct((META_ROWS, n_tok), F32),
            jax.ShapeDtypeStruct((nseq, POOL_HIST, POOL_W), F32),
            jax.ShapeDtypeStruct((nseq, CONV_HIST, CONV_W), F32),
        ],
        scratch_shapes=[
            pltpu.VMEM((seqs, POOL_HALO + steps, POOL_W), F32),
            pltpu.VMEM((seqs, CONV_HALO + steps, CONV_W), F32),
            pltpu.VMEM((2 * PEER_HEADS, PEER_NK, tokens), F32),
            pltpu.VMEM((2 * PEER_HEADS, PEER_TOPK, tokens), F32),
            pltpu.VMEM((2 * PEER_HEADS, PEER_TOPK, tokens), F32),
            slot_scratch, slot_scratch, slot_scratch, slot_scratch, slot_scratch,
        ],
        compiler_params=pltpu.CompilerParams(
            dimension_semantics=("arbitrary", "arbitrary"),
            vmem_limit_bytes=V7X_VMEM_BYTES * 3 // 4),
        name="mixer_router",
    )(x, pool_hist, conv_hist, *weights)


def _tile_rows(index):
    return pl.ds(pl.multiple_of(index * ROW_TILE, ROW_TILE), ROW_TILE)


def _sum_tiles(p):
    sub = lax.broadcasted_iota(jnp.int32, (V7X_SUBLANES, V7X_LANES), 0)
    m4 = sub < 4
    m2 = (sub & 2) == 0
    m1 = (sub & 1) == 0
    b = [jnp.where(m4, p[i], p[i + 4]) + pltpu.roll(jnp.where(m4, p[i + 4], p[i]), 4, 0)
         for i in range(4)]
    c = [jnp.where(m2, b[i], pltpu.roll(b[i + 2], 2, 0))
         + jnp.where(m2, pltpu.roll(b[i], 6, 0), b[i + 2]) for i in range(2)]
    return (jnp.where(m1, c[0], pltpu.roll(c[1], 1, 0))
            + jnp.where(m1, pltpu.roll(c[0], 7, 0), c[1]))


def _tree_sum(terms):
    while len(terms) > 1:
        terms = [terms[i] + terms[i + 1] for i in range(0, len(terms), 2)]
    return terms[0]


def _peer_experts_kernel(meta_ref, gate_ref, h2_ref, acc_ref, u_ref, v_ref, out_ref,
                         pair_scr, gate_scr, group_tab):
    chunk = pl.program_id(0)

    @pl.when((chunk == 0) & (pl.program_id(1) == 0))
    def _():
        pair_scr[...] = jnp.zeros(pair_scr.shape, F32)
        gate_scr[...] = jnp.zeros(gate_scr.shape, F32)

    sub = lax.broadcasted_iota(jnp.int32, (V7X_SUBLANES, V7X_LANES), 0)
    lane = lax.broadcasted_iota(jnp.int32, (V7X_SUBLANES, V7X_LANES), 1)

    def list_token(t, count):
        bounds = t * META_ROWS + GSTART_ROW + chunk
        first = meta_ref[bounds]
        for j in range(MAX_TOKEN_GROUPS):
            group_tab[count + j] = t * META_ROWS + (first + j) * GROUP
        return count + meta_ref[bounds + 1] - first

    n_groups = lax.fori_loop(0, PEER_BLOCK, list_token, 0)
    for q in range(GROUPS_PER_STEP - 1):
        group_tab[n_groups + q] = ZERO_ENTRY
    n_steps = (n_groups + GROUPS_PER_STEP - 1) // GROUPS_PER_STEP

    def token_of(entry):
        return lax.shift_right_logical(entry, META_ROWS.bit_length() - 1)

    def dots(i, carry):
        for q in range(GROUPS_PER_STEP):
            slot = i * GROUPS_PER_STEP + q
            entry = group_tab[slot]
            h = h2_ref[_tile_rows(token_of(entry)), :]
            for half in range(GROUP // V7X_SUBLANES):
                prods = []
                for j in range(V7X_SUBLANES):
                    off = meta_ref[entry + half * V7X_SUBLANES + j]
                    prods.append(u_ref[pl.ds(pl.multiple_of(off, ROW_TILE), ROW_TILE), :] * h)
                pair_scr[_tile_rows(slot * 2 + half), :] = _sum_tiles(prods)
                first = entry + half * V7X_SUBLANES
                gates = gate_ref[pl.ds(lax.shift_right_logical(first, 7), 1), :]
                own = lane == sub + (first & (V7X_LANES - 1))
                gate_scr[_tile_rows(slot * 2 + half), :] = jnp.where(own, gates, 0.0)
        return carry

    lax.fori_loop(0, n_steps, dots, 0)

    def activate(i, carry):
        rows = pl.ds(pl.multiple_of(i * SLAB_ROWS, SLAB_ROWS), SLAB_ROWS)
        act = jnp.sum(pair_scr[rows, :], axis=-1, keepdims=True)
        gate = jnp.sum(gate_scr[rows, :], axis=-1, keepdims=True)
        pair_scr[rows, :] = jnp.broadcast_to(gate * _gelu_tanh(act), (SLAB_ROWS, V7X_LANES))
        return carry

    n_rows = n_steps * (GROUPS_PER_STEP * GROUP)
    lax.fori_loop(0, (n_rows + SLAB_ROWS - 1) // SLAB_ROWS, activate, 0)

    out_ref[...] = acc_ref[...]

    def mix(i, carry):
        for q in range(GROUPS_PER_STEP):
            slot = i * GROUPS_PER_STEP + q
            entry = group_tab[slot]
            terms = []
            for half in range(GROUP // V7X_SUBLANES):
                coef = pair_scr[_tile_rows(slot * 2 + half), :]
                for j in range(V7X_SUBLANES):
                    k = half * V7X_SUBLANES + j
                    off = meta_ref[entry + k]
                    cj = jnp.broadcast_to(coef[j:j + 1, :], (V7X_SUBLANES, V7X_LANES))
                    v_row = v_ref[pl.ds(pl.multiple_of(off, ROW_TILE), ROW_TILE), :]
                    terms.append(cj * v_row)
            rows = _tile_rows(token_of(entry))
            out_ref[rows, :] = out_ref[rows, :] + _tree_sum(terms)
        return carry

    lax.fori_loop(0, n_steps, mix, 0)


def _peer_experts(meta, pgate, h2_tiles, peer_u, peer_v):
    n_tok = meta.shape[0] // META_ROWS
    tok_spec = pl.BlockSpec((PEER_BLOCK * ROW_TILE, V7X_LANES), lambda c, i: (i, 0))
    table_spec = pl.BlockSpec((CHUNK_EXPERTS * ROW_TILE, V7X_LANES), lambda c, i: (c, 0),
                              pipeline_mode=pl.Buffered(1))
    max_groups = (PEER_BLOCK + 1) * MAX_TOKEN_GROUPS + GROUPS_PER_STEP
    pair_rows = -(-max_groups * GROUP // SLAB_ROWS) * SLAB_ROWS
    table_bytes = 2 * CHUNK_EXPERTS * D_MODEL * 4
    block_bytes = PEER_BLOCK * D_MODEL * 4
    vmem_bytes = table_bytes + 2 * pair_rows * V7X_LANES * 4 + 7 * block_bytes + (4 << 20)
    assert vmem_bytes <= V7X_VMEM_BYTES
    gate_spec = pl.BlockSpec((PEER_BLOCK * META_ROWS // V7X_LANES, V7X_LANES), lambda c, i: (i, 0))
    return pl.pallas_call(
        _peer_experts_kernel,
        grid=(PEER_CHUNKS, n_tok // PEER_BLOCK),
        in_specs=[
            pl.BlockSpec((PEER_BLOCK * META_ROWS,), lambda c, i: (i,), memory_space=pltpu.SMEM),
            gate_spec, tok_spec, tok_spec, table_spec, table_spec,
        ],
        out_specs=tok_spec,
        out_shape=jax.ShapeDtypeStruct((n_tok * ROW_TILE, V7X_LANES), F32),
        scratch_shapes=[pltpu.VMEM((pair_rows, V7X_LANES), F32),
                        pltpu.VMEM((pair_rows, V7X_LANES), F32),
                        pltpu.SMEM((max_groups,), jnp.int32)],
        input_output_aliases={3: 0},
        compiler_params=pltpu.CompilerParams(
            dimension_semantics=("arbitrary", "arbitrary"),
            vmem_limit_bytes=vmem_bytes),
        name="peer_experts",
    )(meta, pgate, h2_tiles, jnp.zeros((n_tok * ROW_TILE, V7X_LANES), F32), peer_u, peer_v)


def _final_norm_kernel(x1_ref, peer_ref, g_ref, y_ref):
    y_ref[...] = _rms(x1_ref[...] + peer_ref[...], g_ref[...])


def _final_norm(x1, peer_out, g_final):
    n_tok = x1.shape[0]
    spec = pl.BlockSpec((NORM_BLOCK, D_MODEL), lambda i: (i, 0))
    return pl.pallas_call(
        _final_norm_kernel,
        grid=(n_tok // NORM_BLOCK,),
        in_specs=[spec, spec, pl.BlockSpec((1, D_MODEL), lambda i: (0, 0))],
        out_specs=spec,
        out_shape=jax.ShapeDtypeStruct((n_tok, D_MODEL), F32),
        name="final_norm",
    )(x1, peer_out, g_final)


def _trunk(x, pool_hist, conv_hist, pos0, seqs, steps, weights, g_final, u_tiles, v_tiles):
    nseq, length, _ = x.shape
    n_tok = nseq * length
    x1, h2, meta_t, pgate_t, new_pool, new_conv = _mixer_router(
        x, pool_hist, conv_hist, pos0, seqs, steps, weights)
    peer_out = _peer_experts(meta_t.T.reshape(-1), pgate_t.T.reshape(-1, V7X_LANES),
                             h2.reshape(n_tok * ROW_TILE, V7X_LANES), u_tiles, v_tiles)
    y = _final_norm(x1.reshape(n_tok, D_MODEL), peer_out.reshape(n_tok, D_MODEL), g_final)
    return y.reshape(nseq, length, D_MODEL), new_pool[None], new_conv[None]


def kernel(x_prompt, x_sample, state_pool, state_conv, g_mix, w_in, b_in, w_pool, b_pool,
           pool_scale, w_dw, b_dw, ln_g, ln_b, w_out, g_ffn, w_query, sub_keys, peer_u,
           peer_v, g_final):
    assert g_mix.shape[0] == 1, "single-layer trunk"
    wq_hi, wq_lo = _split_bf16(w_query[0])
    keys = sub_keys[0].reshape(2 * PEER_HEADS, PEER_NK, PEER_DKH)
    keys_hi, keys_lo = _split_bf16(keys)
    row = lambda v: v.reshape(1, -1)
    weights = (
        row(g_mix[0]), w_in[0].astype(BF16), row(b_in[0]), w_pool[0].astype(BF16),
        row(b_pool[0]), row(pool_scale[0]), w_dw[0], row(b_dw[0]), row(ln_g[0]),
        row(ln_b[0]), w_out[0].astype(BF16), row(g_ffn[0]), wq_hi, wq_lo, keys_hi, keys_lo)
    g_final_row = row(g_final)
    u_tiles = peer_u[0].reshape(PEER_N * ROW_TILE, V7X_LANES)
    v_tiles = peer_v[0].reshape(PEER_N * ROW_TILE, V7X_LANES)

    n_prompt = x_prompt.shape[0]
    zero_pool = jnp.zeros((n_prompt, POOL_HIST, POOL_W), F32)
    zero_conv = jnp.zeros((n_prompt, CONV_HIST, CONV_W), F32)
    y_p, pool_p, conv_p = _trunk(
        x_prompt, zero_pool, zero_conv, 0, 1, BLOCK_TOKENS, weights, g_final_row,
        u_tiles, v_tiles)
    dec_seq = x_sample.shape[1]
    y_s, pool_s, conv_s = _trunk(
        x_sample, state_pool[0], state_conv[0], PAST_LEN, BLOCK_TOKENS // dec_seq, dec_seq,
        weights, g_final_row, u_tiles, v_tiles)
    return (y_p, y_s, pool_p, conv_p, pool_s, conv_s)
```

```python
import functools
import math

import jax
import jax.numpy as jnp
from jax import lax
from jax.experimental import pallas as pl
from jax.experimental.pallas import tpu as pltpu

D_MODEL = 1024
POOL_W = 512
POOL_WINDOWS = (2, 4, 8, 16)
POOL_GW = POOL_W // len(POOL_WINDOWS)
POOL_HIST = max(POOL_WINDOWS) - 1
CONV_W = 512
CONV_K = 31
CONV_HIST = CONV_K - 1
IN_W = POOL_W + 2 * CONV_W
PEER_HEADS = 8
PEER_NK = 128
PEER_N = PEER_NK * PEER_NK
PEER_DKH = 128
PEER_TOPK = 16
PEER_SLOTS = PEER_HEADS * PEER_TOPK
PAST_LEN = 1024
EPS = 1e-6

V7X_LANES = 128
V7X_SUBLANES = 8
V7X_VMEM_BYTES = 64 * 1024 * 1024

POOL_HALO = 16
CONV_HALO = 32

BLOCK_TOKENS = 256
PEER_BLOCK = 128
NORM_BLOCK = 512

ROW_TILE = D_MODEL // V7X_LANES
PEER_CHUNKS = 4
CHUNK_EXPERTS = PEER_N // PEER_CHUNKS
GROUP = 2 * V7X_SUBLANES
LIST_ROWS = 192
META_ROWS = 256
GSTART_ROW = LIST_ROWS
SLAB_ROWS = 512
GROUPS_PER_STEP = 4
BATCH_STEPS = 16
MAX_TOKEN_GROUPS = PEER_SLOTS // GROUP
TABLE_RUN = 4
ZERO_ENTRY = GSTART_ROW + GROUP
assert ZERO_ENTRY >= GSTART_ROW + V7X_SUBLANES and ZERO_ENTRY + GROUP <= META_ROWS
assert META_ROWS & (META_ROWS - 1) == 0
assert LIST_ROWS >= PEER_SLOTS + PEER_CHUNKS * (GROUP - 1)
assert ROW_TILE == V7X_SUBLANES

F32 = jnp.float32
BF16 = jnp.bfloat16


def _rms(x, g):
    return x * lax.rsqrt(jnp.mean(x * x, axis=-1, keepdims=True) + EPS) * g


def _sigmoid(x):
    return 1.0 / (1.0 + jnp.exp(-x))


def _gelu_tanh(x):
    c = math.sqrt(2.0 / math.pi)
    return 0.5 * x * (1.0 + jnp.tanh(c * (x + 0.044715 * (x * x * x))))


def _split_bf16(a):
    hi = a.astype(BF16)
    lo = (a - hi.astype(F32)).astype(BF16)
    return hi, lo


def _dot(a, b):
    return jnp.dot(a, b, preferred_element_type=F32)


def _dot_nt(a, b):
    return lax.dot_general(a, b, (((1,), (1,)), ((), ())), preferred_element_type=F32)


def _row_iota(shape):
    return lax.broadcasted_iota(jnp.int32, shape, 0).astype(F32)


def _topk_rows(s, k, ids=None):
    n, width = s.shape
    if ids is None:
        ids = _row_iota((n, width))
    out_rows = lax.broadcasted_iota(jnp.int32, (k, width), 0)
    vals = jnp.zeros((k, width), F32)
    idxs = jnp.zeros((k, width), F32)
    for j in range(k):
        m = jnp.max(s, axis=0, keepdims=True)
        i = jnp.min(jnp.where(s == m, ids, jnp.inf), axis=0, keepdims=True)
        vals = jnp.where(out_rows == j, m, vals)
        idxs = jnp.where(out_rows == j, i, idxs)
        s = jnp.where(ids == i, -jnp.inf, s)
    return vals, idxs


def _lookup_rows(table, rows):
    out = jnp.zeros(rows.shape, F32)
    for r in range(table.shape[0]):
        out = jnp.where(rows == float(r), table[r:r + 1, :], out)
    return out


def _pair_candidates(s1, s2):
    width = s1.shape[1]
    sub = lax.broadcasted_iota(jnp.int32, (V7X_SUBLANES, width), 0)
    subf = sub.astype(F32)
    low = sub < 4
    s2_lo, s2_hi = s2[0:8, :], s2[8:16, :]
    s2_lo_twice = jnp.where(low, s2_lo, pltpu.roll(s2_lo, 4, 0))
    b_twice = jnp.where(low, subf, subf - 4.0)
    sums = [s1[0:1, :] + s2_lo, s1[0:1, :] + s2_hi]
    ids = [subf, subf + 8.0]
    for a in (1, 2, 3):
        sums.append(s1[a:a + 1, :] + s2_lo)
        ids.append(subf + 16.0 * a)
    for a in (4, 6):
        sums.append(jnp.where(low, s1[a:a + 1, :], s1[a + 1:a + 2, :]) + s2_lo_twice)
        ids.append(jnp.where(low, 16.0 * a, 16.0 * (a + 1)) + b_twice)
    sums.append(s1[8:16, :] + s2[0:1, :])
    ids.append((subf + 8.0) * 16.0)
    return jnp.concatenate(sums, axis=0), jnp.concatenate(ids, axis=0)


def _mixer_router_kernel(
        x_ref, ph_ref, ch_ref, g_mix_ref, w_in_ref, b_in_ref, w_pool_ref, b_pool_ref,
        pool_scale_ref, w_dw_ref, b_dw_ref, ln_g_ref, ln_b_ref, w_out_ref, g_ffn_ref,
        wq_hi_ref, wq_lo_ref, keys_hi_ref, keys_lo_ref,
        x1_ref, h2_ref, meta_ref, pgate_ref, new_pool_ref, new_conv_ref,
        zbuf, ubuf, s_scr, top_s, top_i, e_scr, g_scr, key_scr, p_scr, off_scr,
        *, pos0, seqs, steps):
    tb = pl.program_id(1)
    tokens = seqs * steps

    @pl.when(tb == 0)
    def _():
        zbuf[:, POOL_HALO - POOL_HIST:POOL_HALO, :] = ph_ref[...]
        ubuf[:, CONV_HALO - CONV_HIST:CONV_HALO, :] = ch_ref[...]

    x = x_ref[...].reshape(tokens, D_MODEL)
    h = _rms(x, g_mix_ref[...])
    z = _dot(h.astype(BF16), w_in_ref[...]) + b_in_ref[...]
    za = z[:, :POOL_W]
    zv = z[:, POOL_W:POOL_W + CONV_W]
    zg = z[:, POOL_W + CONV_W:]
    u = zv * _sigmoid(zg)
    zbuf[:, POOL_HALO:, :] = za.reshape(seqs, steps, POOL_W)
    ubuf[:, CONV_HALO:, :] = u.reshape(seqs, steps, CONV_W)

    pos = pos0 + tb * steps + lax.broadcasted_iota(jnp.int32, (steps, 1), 0)

    a_parts, c_parts = [], []
    for s in range(seqs):
        groups = []
        for g, w in enumerate(POOL_WINDOWS):
            cols = slice(g * POOL_GW, (g + 1) * POOL_GW)
            tot = zbuf[s, POOL_HALO:POOL_HALO + steps, cols]
            for j in range(1, w):
                tot = tot + zbuf[s, POOL_HALO - j:POOL_HALO - j + steps, cols]
            cnt = jnp.minimum(pos + 1, w).astype(F32)
            d = tot / cnt - zbuf[s, POOL_HALO:POOL_HALO + steps, cols]
            groups.append(_dot(d.astype(BF16), w_pool_ref[g]))
        a_parts.append(jnp.concatenate(groups, axis=-1))

        conv = jnp.zeros((steps, CONV_W), F32)
        for k in range(CONV_K):
            lo = CONV_HALO - CONV_HIST + k
            conv = conv + ubuf[s, lo:lo + steps, :] * w_dw_ref[k:k + 1, :]
        c_parts.append(conv)

    a = jnp.concatenate(a_parts, axis=0) if seqs > 1 else a_parts[0]
    a = (a + b_pool_ref[...]) * pool_scale_ref[...]
    c = jnp.concatenate(c_parts, axis=0) if seqs > 1 else c_parts[0]
    c = c + b_dw_ref[...]
    mu = jnp.mean(c, axis=-1, keepdims=True)
    var = jnp.mean(jnp.square(c - mu), axis=-1, keepdims=True)
    c = (c - mu) * lax.rsqrt(var + EPS) * ln_g_ref[...] + ln_b_ref[...]
    c = c * _sigmoid(c)

    x1 = (x + _dot(a.astype(BF16), w_out_ref[:POOL_W, :])
          + _dot(c.astype(BF16), w_out_ref[POOL_W:, :]))
    x1_ref[...] = x1.reshape(seqs, steps, D_MODEL)

    new_pool_ref[...] = zbuf[:, steps + POOL_HALO - POOL_HIST:steps + POOL_HALO, :]
    new_conv_ref[...] = ubuf[:, steps + CONV_HALO - CONV_HIST:steps + CONV_HALO, :]
    zbuf[:, POOL_HALO - POOL_HIST:POOL_HALO, :] = (
        zbuf[:, steps + POOL_HALO - POOL_HIST:steps + POOL_HALO, :])
    ubuf[:, CONV_HALO - CONV_HIST:CONV_HALO, :] = (
        ubuf[:, steps + CONV_HALO - CONV_HIST:steps + CONV_HALO, :])

    h2 = _rms(x1, g_ffn_ref[...])
    h2_ref[...] = h2
    h2_hi, h2_lo = _split_bf16(h2)
    q = (_dot(h2_hi, wq_hi_ref[...]) + _dot(h2_lo, wq_hi_ref[...])
         + _dot(h2_hi, wq_lo_ref[...]))
    for l in range(2 * PEER_HEADS):
        q_hi, q_lo = _split_bf16(q[:, l * PEER_DKH:(l + 1) * PEER_DKH])
        k_hi = keys_hi_ref[l]
        k_lo = keys_lo_ref[l]
        s_scr[l] = _dot_nt(k_hi, q_hi) + _dot_nt(k_hi, q_lo) + _dot_nt(k_lo, q_hi)

    lane_groups = tokens // V7X_LANES

    def stage1(l, carry):
        for lg in range(lane_groups):
            lanes = slice(lg * V7X_LANES, (lg + 1) * V7X_LANES)
            vals, idxs = _topk_rows(s_scr[l, :, lanes], PEER_TOPK)
            top_s[l, :, lanes] = vals
            top_i[l, :, lanes] = idxs
        return carry

    lax.fori_loop(0, 2 * PEER_HEADS, stage1, 0)

    def stage2(hd, carry):
        rows = pl.ds(pl.multiple_of(hd * PEER_TOPK, PEER_TOPK), PEER_TOPK)
        for lg in range(lane_groups):
            lanes = slice(lg * V7X_LANES, (lg + 1) * V7X_LANES)
            comb, flat = _pair_candidates(top_s[2 * hd, :, lanes], top_s[2 * hd + 1, :, lanes])
            f_s, f_i = _topk_rows(comb, PEER_TOPK, flat)
            ra = jnp.floor(f_i * (1.0 / PEER_TOPK))
            rb = f_i - ra * PEER_TOPK
            eidx = (_lookup_rows(top_i[2 * hd, :, lanes], ra) * PEER_NK
                    + _lookup_rows(top_i[2 * hd + 1, :, lanes], rb))
            p = jnp.exp(f_s - jnp.max(f_s, axis=0, keepdims=True))
            e_scr[rows, lanes] = eidx
            g_scr[rows, lanes] = p / jnp.sum(p, axis=0, keepdims=True)
        return carry

    lax.fori_loop(0, PEER_HEADS, stage2, 0)

    def stage3(lg):
        lanes = slice(lg * V7X_LANES, (lg + 1) * V7X_LANES)
        e = e_scr[:, lanes]
        slot = _row_iota((PEER_SLOTS, V7X_LANES))
        key = e * PEER_SLOTS + slot
        key_scr[:, lanes] = key

        def tile_of(ref, kb):
            return ref[pl.ds(pl.multiple_of(kb * V7X_SUBLANES, V7X_SUBLANES), V7X_SUBLANES), lanes]

        def rank_body(kb, rank):
            keys8 = tile_of(key_scr, kb)
            for j in range(V7X_SUBLANES):
                rank = rank + jnp.where(keys8[j:j + 1, :] < key, 1.0, 0.0)
            return rank

        rank = lax.fori_loop(0, PEER_SLOTS // V7X_SUBLANES, rank_body, jnp.zeros_like(key))
        chunk = jnp.floor(e * (1.0 / CHUNK_EXPERTS))
        start = jnp.zeros((1, V7X_LANES), F32)
        pstart = jnp.zeros((1, V7X_LANES), F32)
        shift = jnp.zeros_like(key)
        grow = lax.broadcasted_iota(jnp.int32, (V7X_SUBLANES, V7X_LANES), 0)
        gstart = jnp.zeros((V7X_SUBLANES, V7X_LANES), F32)
        for ch in range(PEER_CHUNKS):
            mine = chunk == float(ch)
            n = jnp.sum(jnp.where(mine, 1.0, 0.0), axis=0, keepdims=True)
            shift = jnp.where(mine, pstart - start, shift)
            start = start + n
            pstart = pstart + jnp.ceil(n * (1.0 / GROUP)) * GROUP
            gstart = jnp.where(grow == ch + 1, pstart * (1.0 / GROUP), gstart)
        p_scr[:, lanes] = rank + shift
        off_scr[:, lanes] = (e - chunk * CHUNK_EXPERTS) * (2 * ROW_TILE)

        list_rows = _row_iota((LIST_ROWS, V7X_LANES))

        def place(kb, lists):
            offs, gates = lists
            p8, off8, gate8 = tile_of(p_scr, kb), tile_of(off_scr, kb), tile_of(g_scr, kb)
            for j in range(V7X_SUBLANES):
                hit = list_rows == p8[j:j + 1, :]
                offs = jnp.where(hit, off8[j:j + 1, :], offs)
                gates = jnp.where(hit, gate8[j:j + 1, :], gates)
            return offs, gates

        zeros = jnp.zeros((LIST_ROWS, V7X_LANES), F32)
        offs, gates = lax.fori_loop(0, PEER_SLOTS // V7X_SUBLANES, place, (zeros, zeros))
        meta_ref[0:LIST_ROWS, lanes] = offs.astype(jnp.int32)
        meta_ref[GSTART_ROW:GSTART_ROW + V7X_SUBLANES, lanes] = gstart.astype(jnp.int32)
        meta_ref[GSTART_ROW + V7X_SUBLANES:, lanes] = jnp.zeros(
            (META_ROWS - GSTART_ROW - V7X_SUBLANES, V7X_LANES), jnp.int32)
        pgate_ref[0:LIST_ROWS, lanes] = gates
        pgate_ref[LIST_ROWS:, lanes] = jnp.zeros((META_ROWS - LIST_ROWS, V7X_LANES), F32)

    for lg in range(lane_groups):
        stage3(lg)


def _const_spec(shape):
    zeros = (0,) * len(shape)
    return pl.BlockSpec(shape, lambda i, j: zeros, pipeline_mode=pl.Buffered(1))


def _mixer_router(x, pool_hist, conv_hist, pos0, seqs, steps, weights):
    nseq, length, _ = x.shape
    tokens = seqs * steps
    grid = (nseq // seqs, length // steps)
    blocks_per_seq_group = length // steps
    per_time = pl.BlockSpec((seqs, steps, D_MODEL), lambda i, j: (i, j, 0))
    per_token = pl.BlockSpec((tokens, D_MODEL), lambda i, j: (i * blocks_per_seq_group + j, 0))
    per_seq_pool = pl.BlockSpec((seqs, POOL_HIST, POOL_W), lambda i, j: (i, 0, 0))
    per_seq_conv = pl.BlockSpec((seqs, CONV_HIST, CONV_W), lambda i, j: (i, 0, 0))

    def slot_spec(rows):
        return pl.BlockSpec((rows, tokens), lambda i, j: (0, i * blocks_per_seq_group + j))

    n_tok = nseq * length
    kernel = functools.partial(_mixer_router_kernel, pos0=pos0, seqs=seqs, steps=steps)
    slot_scratch = pltpu.VMEM((PEER_SLOTS, tokens), F32)
    return pl.pallas_call(
        kernel,
        grid=grid,
        in_specs=[per_time, per_seq_pool, per_seq_conv] + [_const_spec(w.shape) for w in weights],
        out_specs=[per_time, per_token, slot_spec(META_ROWS), slot_spec(META_ROWS),
                   per_seq_pool, per_seq_conv],
        out_shape=[
            jax.ShapeDtypeStruct((nseq, length, D_MODEL), F32),
            jax.ShapeDtypeStruct((n_tok, D_MODEL), F32),
            jax.ShapeDtypeStruct((META_ROWS, n_tok), jnp.int32),
            jax.ShapeDtypeStruct((META_ROWS, n_tok), F32),
            jax.ShapeDtypeStruct((nseq, POOL_HIST, POOL_W), F32),
            jax.ShapeDtypeStruct((nseq, CONV_HIST, CONV_W), F32),
        ],
        scratch_shapes=[
            pltpu.VMEM((seqs, POOL_HALO + steps, POOL_W), F32),
            pltpu.VMEM((seqs, CONV_HALO + steps, CONV_W), F32),
            pltpu.VMEM((2 * PEER_HEADS, PEER_NK, tokens), F32),
            pltpu.VMEM((2 * PEER_HEADS, PEER_TOPK, tokens), F32),
            pltpu.VMEM((2 * PEER_HEADS, PEER_TOPK, tokens), F32),
            slot_scratch, slot_scratch, slot_scratch, slot_scratch, slot_scratch,
        ],
        compiler_params=pltpu.CompilerParams(
            dimension_semantics=("arbitrary", "arbitrary"),
            vmem_limit_bytes=V7X_VMEM_BYTES * 3 // 4),
        name="mixer_router",
    )(x, pool_hist, conv_hist, *weights)


def _tile_rows(index):
    return pl.ds(pl.multiple_of(index * ROW_TILE, ROW_TILE), ROW_TILE)


def _sum_tiles(p):
    sub = lax.broadcasted_iota(jnp.int32, (V7X_SUBLANES, V7X_LANES), 0)
    m4 = sub < 4
    m2 = (sub & 2) == 0
    m1 = (sub & 1) == 0
    b = [jnp.where(m4, p[i], p[i + 4]) + pltpu.roll(jnp.where(m4, p[i + 4], p[i]), 4, 0)
         for i in range(4)]
    c = [jnp.where(m2, b[i], pltpu.roll(b[i + 2], 2, 0))
         + jnp.where(m2, pltpu.roll(b[i], 6, 0), b[i + 2]) for i in range(2)]
    return (jnp.where(m1, c[0], pltpu.roll(c[1], 1, 0))
            + jnp.where(m1, pltpu.roll(c[0], 7, 0), c[1]))


def _tree_sum(terms):
    while len(terms) > 1:
        terms = [terms[i] + terms[i + 1] for i in range(0, len(terms), 2)]
    return terms[0]


def _peer_experts_kernel(meta_ref, gate_ref, h2_ref, acc_ref, uv_ref, out_ref,
                         pair_scr, gate_scr, v_scr, group_tab):
    chunk = pl.program_id(0)

    @pl.when((chunk == 0) & (pl.program_id(1) == 0))
    def _():
        pair_scr[...] = jnp.zeros(pair_scr.shape, F32)
        gate_scr[...] = jnp.zeros(gate_scr.shape, F32)

    sub = lax.broadcasted_iota(jnp.int32, (V7X_SUBLANES, V7X_LANES), 0)
    lane = lax.broadcasted_iota(jnp.int32, (V7X_SUBLANES, V7X_LANES), 1)
    lane_minus_sub = [lane - sub - half * V7X_SUBLANES for half in range(GROUP // V7X_SUBLANES)]

    def list_token(t, count):
        bounds = t * META_ROWS + GSTART_ROW + chunk
        first = meta_ref[bounds]
        mine = meta_ref[bounds + 1] - first
        first_entry = t * META_ROWS + first * GROUP
        for run in range(MAX_TOKEN_GROUPS // TABLE_RUN):
            def write_run(run=run):
                for j in range(run * TABLE_RUN, (run + 1) * TABLE_RUN):
                    group_tab[count + j] = first_entry + j * GROUP
            if run == 0:
                write_run()
            else:
                pl.when(mine > run * TABLE_RUN)(write_run)
        return count + mine

    n_groups = lax.fori_loop(0, PEER_BLOCK, list_token, 0)
    for q in range(GROUPS_PER_STEP - 1):
        group_tab[n_groups + q] = ZERO_ENTRY
    n_steps = (n_groups + GROUPS_PER_STEP - 1) // GROUPS_PER_STEP

    def token_of(entry):
        return lax.shift_right_logical(entry, META_ROWS.bit_length() - 1)

    out_ref[...] = acc_ref[...]

    def batch(b, carry):
        step0 = b * BATCH_STEPS
        step1 = jnp.minimum(step0 + BATCH_STEPS, n_steps)

        def dots(i, carry):
            for q in range(GROUPS_PER_STEP):
                entry = group_tab[i * GROUPS_PER_STEP + q]
                slot = (i - step0) * GROUPS_PER_STEP + q
                h = h2_ref[_tile_rows(token_of(entry)), :]
                gates = gate_ref[pl.ds(lax.shift_right_logical(entry, 7), 1), :]
                first_lane = entry & (V7X_LANES - 1)
                for half in range(GROUP // V7X_SUBLANES):
                    prods = []
                    for j in range(V7X_SUBLANES):
                        k = half * V7X_SUBLANES + j
                        off = meta_ref[entry + k]
                        uv = uv_ref[pl.ds(pl.multiple_of(off, 2 * ROW_TILE), 2 * ROW_TILE), :]
                        prods.append(uv[:ROW_TILE, :] * h)
                        v_scr[_tile_rows(slot * GROUP + k), :] = uv[ROW_TILE:, :]
                    pair_scr[_tile_rows(slot * 2 + half), :] = _sum_tiles(prods)
                    own = lane_minus_sub[half] == first_lane
                    gate_scr[_tile_rows(slot * 2 + half), :] = jnp.where(own, gates, 0.0)
            return carry

        lax.fori_loop(step0, step1, dots, 0)

        def activate(i, carry):
            rows = pl.ds(pl.multiple_of(i * SLAB_ROWS, SLAB_ROWS), SLAB_ROWS)
            act = jnp.sum(pair_scr[rows, :], axis=-1, keepdims=True)
            gate = jnp.sum(gate_scr[rows, :], axis=-1, keepdims=True)
            pair_scr[rows, :] = jnp.broadcast_to(gate * _gelu_tanh(act), (SLAB_ROWS, V7X_LANES))
            return carry

        n_rows = (step1 - step0) * (GROUPS_PER_STEP * GROUP)
        lax.fori_loop(0, (n_rows + SLAB_ROWS - 1) // SLAB_ROWS, activate, 0)

        def mix(i, carry):
            for q in range(GROUPS_PER_STEP):
                entry = group_tab[i * GROUPS_PER_STEP + q]
                slot = (i - step0) * GROUPS_PER_STEP + q
                terms = []
                for half in range(GROUP // V7X_SUBLANES):
                    coef = pair_scr[_tile_rows(slot * 2 + half), :]
                    for j in range(V7X_SUBLANES):
                        k = half * V7X_SUBLANES + j
                        cj = jnp.broadcast_to(coef[j:j + 1, :], (V7X_SUBLANES, V7X_LANES))
                        terms.append(cj * v_scr[_tile_rows(slot * GROUP + k), :])
                rows = _tile_rows(token_of(entry))
                out_ref[rows, :] = out_ref[rows, :] + _tree_sum(terms)
            return carry

        lax.fori_loop(step0, step1, mix, 0)
        return carry

    lax.fori_loop(0, (n_steps + BATCH_STEPS - 1) // BATCH_STEPS, batch, 0)


def _peer_experts(meta, pgate, h2_tiles, peer_uv):
    n_tok = meta.shape[0] // META_ROWS
    tok_spec = pl.BlockSpec((PEER_BLOCK * ROW_TILE, V7X_LANES), lambda c, i: (i, 0))
    table_spec = pl.BlockSpec((CHUNK_EXPERTS * 2 * ROW_TILE, V7X_LANES), lambda c, i: (c, 0),
                              pipeline_mode=pl.Buffered(1))
    max_groups = (PEER_BLOCK + 1) * MAX_TOKEN_GROUPS + GROUPS_PER_STEP
    batch_groups = BATCH_STEPS * GROUPS_PER_STEP
    pair_rows = batch_groups * GROUP
    assert pair_rows % SLAB_ROWS == 0
    v_rows = pair_rows * ROW_TILE
    table_bytes = 2 * CHUNK_EXPERTS * D_MODEL * 4
    block_bytes = PEER_BLOCK * D_MODEL * 4
    vmem_bytes = (table_bytes + (2 * pair_rows + v_rows) * V7X_LANES * 4 + 7 * block_bytes
                  + (4 << 20))
    assert vmem_bytes <= V7X_VMEM_BYTES
    gate_spec = pl.BlockSpec((PEER_BLOCK * META_ROWS // V7X_LANES, V7X_LANES), lambda c, i: (i, 0))
    return pl.pallas_call(
        _peer_experts_kernel,
        grid=(PEER_CHUNKS, n_tok // PEER_BLOCK),
        in_specs=[
            pl.BlockSpec((PEER_BLOCK * META_ROWS,), lambda c, i: (i,), memory_space=pltpu.SMEM),
            gate_spec, tok_spec, tok_spec, table_spec,
        ],
        out_specs=tok_spec,
        out_shape=jax.ShapeDtypeStruct((n_tok * ROW_TILE, V7X_LANES), F32),
        scratch_shapes=[pltpu.VMEM((pair_rows, V7X_LANES), F32),
                        pltpu.VMEM((pair_rows, V7X_LANES), F32),
                        pltpu.VMEM((v_rows, V7X_LANES), F32),
                        pltpu.SMEM((max_groups,), jnp.int32)],
        input_output_aliases={3: 0},
        compiler_params=pltpu.CompilerParams(
            dimension_semantics=("arbitrary", "arbitrary"),
            vmem_limit_bytes=vmem_bytes),
        name="peer_experts",
    )(meta, pgate, h2_tiles, jnp.zeros((n_tok * ROW_TILE, V7X_LANES), F32), peer_uv)


def _final_norm_kernel(x1_ref, peer_ref, g_ref, y_ref):
    y_ref[...] = _rms(x1_ref[...] + peer_ref[...], g_ref[...])


def _final_norm(x1, peer_out, g_final):
    n_tok = x1.shape[0]
    spec = pl.BlockSpec((NORM_BLOCK, D_MODEL), lambda i: (i, 0))
    return pl.pallas_call(
        _final_norm_kernel,
        grid=(n_tok // NORM_BLOCK,),
        in_specs=[spec, spec, pl.BlockSpec((1, D_MODEL), lambda i: (0, 0))],
        out_specs=spec,
        out_shape=jax.ShapeDtypeStruct((n_tok, D_MODEL), F32),
        name="final_norm",
    )(x1, peer_out, g_final)


def _trunk(x, pool_hist, conv_hist, pos0, seqs, steps, weights, g_final, uv_tiles):
    nseq, length, _ = x.shape
    n_tok = nseq * length
    x1, h2, meta_t, pgate_t, new_pool, new_conv = _mixer_router(
        x, pool_hist, conv_hist, pos0, seqs, steps, weights)
    peer_out = _peer_experts(meta_t.T.reshape(-1), pgate_t.T.reshape(-1, V7X_LANES),
                             h2.reshape(n_tok * ROW_TILE, V7X_LANES), uv_tiles)
    y = _final_norm(x1.reshape(n_tok, D_MODEL), peer_out.reshape(n_tok, D_MODEL), g_final)
    return y.reshape(nseq, length, D_MODEL), new_pool[None], new_conv[None]


def kernel(x_prompt, x_sample, state_pool, state_conv, g_mix, w_in, b_in, w_pool, b_pool,
           pool_scale, w_dw, b_dw, ln_g, ln_b, w_out, g_ffn, w_query, sub_keys, peer_u,
           peer_v, g_final):
    assert g_mix.shape[0] == 1, "single-layer trunk"
    wq_hi, wq_lo = _split_bf16(w_query[0])
    keys = sub_keys[0].reshape(2 * PEER_HEADS, PEER_NK, PEER_DKH)
    keys_hi, keys_lo = _split_bf16(keys)
    row = lambda v: v.reshape(1, -1)
    weights = (
        row(g_mix[0]), w_in[0].astype(BF16), row(b_in[0]), w_pool[0].astype(BF16),
        row(b_pool[0]), row(pool_scale[0]), w_dw[0], row(b_dw[0]), row(ln_g[0]),
        row(ln_b[0]), w_out[0].astype(BF16), row(g_ffn[0]), wq_hi, wq_lo, keys_hi, keys_lo)
    g_final_row = row(g_final)
    uv_tiles = jnp.stack(
        [peer_u[0].reshape(PEER_N, ROW_TILE, V7X_LANES),
         peer_v[0].reshape(PEER_N, ROW_TILE, V7X_LANES)], axis=1,
    ).reshape(PEER_N * 2 * ROW_TILE, V7X_LANES)

    n_prompt = x_prompt.shape[0]
    zero_pool = jnp.zeros((n_prompt, POOL_HIST, POOL_W), F32)
    zero_conv = jnp.zeros((n_prompt, CONV_HIST, CONV_W), F32)
    y_p, pool_p, conv_p = _trunk(
        x_prompt, zero_pool, zero_conv, 0, 1, BLOCK_TOKENS, weights, g_final_row, uv_tiles)
    dec_seq = x_sample.shape[1]
    y_s, pool_s, conv_s = _trunk(
        x_sample, state_pool[0], state_conv[0], PAST_LEN, BLOCK_TOKENS // dec_seq, dec_seq,
        weights, g_final_row, uv_tiles)
    return (y_p, y_s, pool_p, conv_p, pool_s, conv_s)
```

```python
import functools
import math

import jax
import jax.numpy as jnp
from jax import lax
from jax.experimental import pallas as pl
from jax.experimental.pallas import tpu as pltpu

D_MODEL = 1024
POOL_W = 512
POOL_WINDOWS = (2, 4, 8, 16)
POOL_GW = POOL_W // len(POOL_WINDOWS)
POOL_HIST = max(POOL_WINDOWS) - 1
CONV_W = 512
CONV_K = 31
CONV_HIST = CONV_K - 1
IN_W = POOL_W + 2 * CONV_W
PEER_HEADS = 8
PEER_NK = 128
PEER_N = PEER_NK * PEER_NK
PEER_DKH = 128
PEER_TOPK = 16
PEER_SLOTS = PEER_HEADS * PEER_TOPK
PAST_LEN = 1024
EPS = 1e-6

V7X_LANES = 128
V7X_SUBLANES = 8
V7X_VMEM_BYTES = 64 * 1024 * 1024

POOL_HALO = 16
CONV_HALO = 32

BLOCK_TOKENS = 256
PEER_BLOCK = 128
NORM_BLOCK = 512

ROW_TILE = D_MODEL // V7X_LANES
PEER_CHUNKS = 4
CHUNK_EXPERTS = PEER_N // PEER_CHUNKS
GROUP = 2 * V7X_SUBLANES
LIST_ROWS = 192
META_ROWS = 256
GSTART_ROW = LIST_ROWS
SLAB_ROWS = 512
GROUPS_PER_STEP = 4
BATCH_STEPS = 16
MAX_TOKEN_GROUPS = PEER_SLOTS // GROUP
TABLE_RUN = 4
ZERO_ENTRY = GSTART_ROW + GROUP
assert ZERO_ENTRY >= GSTART_ROW + V7X_SUBLANES and ZERO_ENTRY + GROUP <= META_ROWS
assert META_ROWS & (META_ROWS - 1) == 0
assert LIST_ROWS >= PEER_SLOTS + PEER_CHUNKS * (GROUP - 1)
COUNT_BITS = 8
assert PEER_SLOTS < (1 << COUNT_BITS) and PEER_CHUNKS * COUNT_BITS <= 32
assert ROW_TILE == V7X_SUBLANES

F32 = jnp.float32
BF16 = jnp.bfloat16


def _rms(x, g):
    return x * lax.rsqrt(jnp.mean(x * x, axis=-1, keepdims=True) + EPS) * g


def _sigmoid(x):
    return 1.0 / (1.0 + jnp.exp(-x))


def _gelu_tanh(x):
    c = math.sqrt(2.0 / math.pi)
    return 0.5 * x * (1.0 + jnp.tanh(c * (x + 0.044715 * (x * x * x))))


def _split_bf16(a):
    hi = a.astype(BF16)
    lo = (a - hi.astype(F32)).astype(BF16)
    return hi, lo


def _dot(a, b):
    return jnp.dot(a, b, preferred_element_type=F32)


def _dot_nt(a, b):
    return lax.dot_general(a, b, (((1,), (1,)), ((), ())), preferred_element_type=F32)


def _row_iota(shape):
    return lax.broadcasted_iota(jnp.int32, shape, 0).astype(F32)


def _topk_rows(s, k, ids=None):
    n, width = s.shape
    if ids is None:
        ids = _row_iota((n, width))
    out_rows = lax.broadcasted_iota(jnp.int32, (k, width), 0)
    vals = jnp.zeros((k, width), F32)
    idxs = jnp.zeros((k, width), F32)
    for j in range(k):
        m = jnp.max(s, axis=0, keepdims=True)
        i = jnp.min(jnp.where(s == m, ids, jnp.inf), axis=0, keepdims=True)
        vals = jnp.where(out_rows == j, m, vals)
        idxs = jnp.where(out_rows == j, i, idxs)
        s = jnp.where(ids == i, -jnp.inf, s)
    return vals, idxs


def _lookup_rows(table, rows):
    out = jnp.zeros(rows.shape, F32)
    for r in range(table.shape[0]):
        out = jnp.where(rows == float(r), table[r:r + 1, :], out)
    return out


def _pair_candidates(s1, s2):
    width = s1.shape[1]
    sub = lax.broadcasted_iota(jnp.int32, (V7X_SUBLANES, width), 0)
    subf = sub.astype(F32)
    low = sub < 4
    s2_lo, s2_hi = s2[0:8, :], s2[8:16, :]
    s2_lo_twice = jnp.where(low, s2_lo, pltpu.roll(s2_lo, 4, 0))
    b_twice = jnp.where(low, subf, subf - 4.0)
    sums = [s1[0:1, :] + s2_lo, s1[0:1, :] + s2_hi]
    ids = [subf, subf + 8.0]
    for a in (1, 2, 3):
        sums.append(s1[a:a + 1, :] + s2_lo)
        ids.append(subf + 16.0 * a)
    for a in (4, 6):
        sums.append(jnp.where(low, s1[a:a + 1, :], s1[a + 1:a + 2, :]) + s2_lo_twice)
        ids.append(jnp.where(low, 16.0 * a, 16.0 * (a + 1)) + b_twice)
    sums.append(s1[8:16, :] + s2[0:1, :])
    ids.append((subf + 8.0) * 16.0)
    return jnp.concatenate(sums, axis=0), jnp.concatenate(ids, axis=0)


def _mixer_router_kernel(
        x_ref, ph_ref, ch_ref, g_mix_ref, w_in_ref, b_in_ref, w_pool_ref, b_pool_ref,
        pool_scale_ref, w_dw_ref, b_dw_ref, ln_g_ref, ln_b_ref, w_out_ref, g_ffn_ref,
        wq_hi_ref, wq_lo_ref, keys_hi_ref, keys_lo_ref,
        x1_ref, h2_ref, meta_ref, pgate_ref, new_pool_ref, new_conv_ref,
        zbuf, ubuf, s_scr, top_s, top_i, e_scr, g_scr, p_scr, off_scr,
        *, pos0, seqs, steps):
    tb = pl.program_id(1)
    tokens = seqs * steps

    @pl.when(tb == 0)
    def _():
        zbuf[:, POOL_HALO - POOL_HIST:POOL_HALO, :] = ph_ref[...]
        ubuf[:, CONV_HALO - CONV_HIST:CONV_HALO, :] = ch_ref[...]

    x = x_ref[...].reshape(tokens, D_MODEL)
    h = _rms(x, g_mix_ref[...])
    z = _dot(h.astype(BF16), w_in_ref[...]) + b_in_ref[...]
    za = z[:, :POOL_W]
    zv = z[:, POOL_W:POOL_W + CONV_W]
    zg = z[:, POOL_W + CONV_W:]
    u = zv * _sigmoid(zg)
    zbuf[:, POOL_HALO:, :] = za.reshape(seqs, steps, POOL_W)
    ubuf[:, CONV_HALO:, :] = u.reshape(seqs, steps, CONV_W)

    pos = pos0 + tb * steps + lax.broadcasted_iota(jnp.int32, (steps, 1), 0)

    a_parts, c_parts = [], []
    for s in range(seqs):
        groups = []
        for g, w in enumerate(POOL_WINDOWS):
            cols = slice(g * POOL_GW, (g + 1) * POOL_GW)
            tot = zbuf[s, POOL_HALO:POOL_HALO + steps, cols]
            for j in range(1, w):
                tot = tot + zbuf[s, POOL_HALO - j:POOL_HALO - j + steps, cols]
            cnt = jnp.minimum(pos + 1, w).astype(F32)
            d = tot / cnt - zbuf[s, POOL_HALO:POOL_HALO + steps, cols]
            groups.append(_dot(d.astype(BF16), w_pool_ref[g]))
        a_parts.append(jnp.concatenate(groups, axis=-1))

        conv = jnp.zeros((steps, CONV_W), F32)
        for k in range(CONV_K):
            lo = CONV_HALO - CONV_HIST + k
            conv = conv + ubuf[s, lo:lo + steps, :] * w_dw_ref[k:k + 1, :]
        c_parts.append(conv)

    a = jnp.concatenate(a_parts, axis=0) if seqs > 1 else a_parts[0]
    a = (a + b_pool_ref[...]) * pool_scale_ref[...]
    c = jnp.concatenate(c_parts, axis=0) if seqs > 1 else c_parts[0]
    c = c + b_dw_ref[...]
    mu = jnp.mean(c, axis=-1, keepdims=True)
    var = jnp.mean(jnp.square(c - mu), axis=-1, keepdims=True)
    c = (c - mu) * lax.rsqrt(var + EPS) * ln_g_ref[...] + ln_b_ref[...]
    c = c * _sigmoid(c)

    x1 = (x + _dot(a.astype(BF16), w_out_ref[:POOL_W, :])
          + _dot(c.astype(BF16), w_out_ref[POOL_W:, :]))
    x1_ref[...] = x1.reshape(seqs, steps, D_MODEL)

    new_pool_ref[...] = zbuf[:, steps + POOL_HALO - POOL_HIST:steps + POOL_HALO, :]
    new_conv_ref[...] = ubuf[:, steps + CONV_HALO - CONV_HIST:steps + CONV_HALO, :]
    zbuf[:, POOL_HALO - POOL_HIST:POOL_HALO, :] = (
        zbuf[:, steps + POOL_HALO - POOL_HIST:steps + POOL_HALO, :])
    ubuf[:, CONV_HALO - CONV_HIST:CONV_HALO, :] = (
        ubuf[:, steps + CONV_HALO - CONV_HIST:steps + CONV_HALO, :])

    h2 = _rms(x1, g_ffn_ref[...])
    for j in range(ROW_TILE):
        h2_ref[pl.ds(j, tokens, stride=ROW_TILE), :] = h2[:, j * V7X_LANES:(j + 1) * V7X_LANES]
    h2_hi, h2_lo = _split_bf16(h2)
    q = (_dot(h2_hi, wq_hi_ref[...]) + _dot(h2_lo, wq_hi_ref[...])
         + _dot(h2_hi, wq_lo_ref[...]))
    for l in range(2 * PEER_HEADS):
        q_hi, q_lo = _split_bf16(q[:, l * PEER_DKH:(l + 1) * PEER_DKH])
        k_hi = keys_hi_ref[l]
        k_lo = keys_lo_ref[l]
        s_scr[l] = _dot_nt(k_hi, q_hi) + _dot_nt(k_hi, q_lo) + _dot_nt(k_lo, q_hi)

    lane_groups = tokens // V7X_LANES

    def stage1(hd, carry):
        for l in (2 * hd, 2 * hd + 1):
            for lg in range(lane_groups):
                lanes = slice(lg * V7X_LANES, (lg + 1) * V7X_LANES)
                vals, idxs = _topk_rows(s_scr[l, :, lanes], PEER_TOPK)
                top_s[l, :, lanes] = vals
                top_i[l, :, lanes] = idxs
        return carry

    lax.fori_loop(0, PEER_HEADS, stage1, 0)

    def stage2(hd, carry):
        rows = pl.ds(pl.multiple_of(hd * PEER_TOPK, PEER_TOPK), PEER_TOPK)
        for lg in range(lane_groups):
            lanes = slice(lg * V7X_LANES, (lg + 1) * V7X_LANES)
            comb, flat = _pair_candidates(top_s[2 * hd, :, lanes], top_s[2 * hd + 1, :, lanes])
            f_s, f_i = _topk_rows(comb, PEER_TOPK, flat)
            ra = jnp.floor(f_i * (1.0 / PEER_TOPK))
            rb = f_i - ra * PEER_TOPK
            eidx = (_lookup_rows(top_i[2 * hd, :, lanes], ra) * PEER_NK
                    + _lookup_rows(top_i[2 * hd + 1, :, lanes], rb))
            p = jnp.exp(f_s - jnp.max(f_s, axis=0, keepdims=True))
            e_scr[rows, lanes] = eidx
            g_scr[rows, lanes] = p / jnp.sum(p, axis=0, keepdims=True)
        return carry

    lax.fori_loop(0, PEER_HEADS, stage2, 0)

    def stage3(lg):
        lanes = slice(lg * V7X_LANES, (lg + 1) * V7X_LANES)
        e = e_scr[:, lanes]
        chunk = jnp.floor(e * (1.0 / CHUNK_EXPERTS))

        def tile_of(ref, kb):
            return ref[pl.ds(pl.multiple_of(kb * V7X_SUBLANES, V7X_SUBLANES), V7X_SUBLANES), lanes]

        field = chunk.astype(jnp.int32) * COUNT_BITS
        ones = jnp.left_shift(jnp.ones_like(field), field)
        slot = lax.broadcasted_iota(jnp.int32, (PEER_SLOTS, V7X_LANES), 0)
        run = ones
        step = 1
        while step < PEER_SLOTS:
            run = run + jnp.where(slot >= step, pltpu.roll(run, step, 0), 0)
            step *= 2
        count_mask = (1 << COUNT_BITS) - 1
        before = (lax.shift_right_logical(run - ones, field) & count_mask).astype(F32)
        totals = run[PEER_SLOTS - 1:PEER_SLOTS, :]
        pstart = jnp.zeros((1, V7X_LANES), F32)
        shift = jnp.zeros_like(e)
        grow = lax.broadcasted_iota(jnp.int32, (V7X_SUBLANES, V7X_LANES), 0)
        gstart = jnp.zeros((V7X_SUBLANES, V7X_LANES), F32)
        for ch in range(PEER_CHUNKS):
            n = (lax.shift_right_logical(totals, ch * COUNT_BITS) & count_mask).astype(F32)
            shift = jnp.where(chunk == float(ch), pstart, shift)
            pstart = pstart + jnp.ceil(n * (1.0 / GROUP)) * GROUP
            gstart = jnp.where(grow == ch + 1, pstart * (1.0 / GROUP), gstart)
        p_scr[:, lanes] = before + shift
        off_scr[:, lanes] = (e - chunk * CHUNK_EXPERTS) * (2 * ROW_TILE)

        list_rows = _row_iota((LIST_ROWS, V7X_LANES))

        def place(kb, lists):
            offs, gates = lists
            p8, off8, gate8 = tile_of(p_scr, kb), tile_of(off_scr, kb), tile_of(g_scr, kb)
            for j in range(V7X_SUBLANES):
                hit = list_rows == p8[j:j + 1, :]
                offs = jnp.where(hit, off8[j:j + 1, :], offs)
                gates = jnp.where(hit, gate8[j:j + 1, :], gates)
            return offs, gates

        zeros = jnp.zeros((LIST_ROWS, V7X_LANES), F32)
        offs, gates = lax.fori_loop(0, PEER_SLOTS // V7X_SUBLANES, place, (zeros, zeros))
        meta_ref[0:LIST_ROWS, lanes] = offs.astype(jnp.int32)
        meta_ref[GSTART_ROW:GSTART_ROW + V7X_SUBLANES, lanes] = gstart.astype(jnp.int32)
        meta_ref[GSTART_ROW + V7X_SUBLANES:, lanes] = jnp.zeros(
            (META_ROWS - GSTART_ROW - V7X_SUBLANES, V7X_LANES), jnp.int32)
        pgate_ref[0:LIST_ROWS, lanes] = gates
        pgate_ref[LIST_ROWS:, lanes] = jnp.zeros((META_ROWS - LIST_ROWS, V7X_LANES), F32)

    for lg in range(lane_groups):
        stage3(lg)


def _const_spec(shape):
    zeros = (0,) * len(shape)
    return pl.BlockSpec(shape, lambda i, j: zeros, pipeline_mode=pl.Buffered(1))


def _mixer_router(x, pool_hist, conv_hist, pos0, seqs, steps, weights):
    nseq, length, _ = x.shape
    tokens = seqs * steps
    grid = (nseq // seqs, length // steps)
    blocks_per_seq_group = length // steps
    per_time = pl.BlockSpec((seqs, steps, D_MODEL), lambda i, j: (i, j, 0))
    per_token = pl.BlockSpec((tokens * ROW_TILE, V7X_LANES),
                             lambda i, j: (i * blocks_per_seq_group + j, 0))
    per_seq_pool = pl.BlockSpec((seqs, POOL_HIST, POOL_W), lambda i, j: (i, 0, 0))
    per_seq_conv = pl.BlockSpec((seqs, CONV_HIST, CONV_W), lambda i, j: (i, 0, 0))

    def slot_spec(rows):
        return pl.BlockSpec((rows, tokens), lambda i, j: (0, i * blocks_per_seq_group + j))

    n_tok = nseq * length
    kernel = functools.partial(_mixer_router_kernel, pos0=pos0, seqs=seqs, steps=steps)
    slot_scratch = pltpu.VMEM((PEER_SLOTS, tokens), F32)
    return pl.pallas_call(
        kernel,
        grid=grid,
        in_specs=[per_time, per_seq_pool, per_seq_conv] + [_const_spec(w.shape) for w in weights],
        out_specs=[per_time, per_token, slot_spec(META_ROWS), slot_spec(META_ROWS),
                   per_seq_pool, per_seq_conv],
        out_shape=[
            jax.ShapeDtypeStruct((nseq, length, D_MODEL), F32),
            jax.ShapeDtypeStruct((n_tok * ROW_TILE, V7X_LANES), F32),
            jax.ShapeDtypeStruct((META_ROWS, n_tok), jnp.int32),
            jax.ShapeDtypeStruct((META_ROWS, n_tok), F32),
            jax.ShapeDtypeStruct((nseq, POOL_HIST, POOL_W), F32),
            jax.ShapeDtypeStruct((nseq, CONV_HIST, CONV_W), F32),
        ],
        scratch_shapes=[
            pltpu.VMEM((seqs, POOL_HALO + steps, POOL_W), F32),
            pltpu.VMEM((seqs, CONV_HALO + steps, CONV_W), F32),
            pltpu.VMEM((2 * PEER_HEADS, PEER_NK, tokens), F32),
            pltpu.VMEM((2 * PEER_HEADS, PEER_TOPK, tokens), F32),
            pltpu.VMEM((2 * PEER_HEADS, PEER_TOPK, tokens), F32),
            slot_scratch, slot_scratch, slot_scratch, slot_scratch,
        ],
        compiler_params=pltpu.CompilerParams(
            dimension_semantics=("arbitrary", "arbitrary"),
            vmem_limit_bytes=V7X_VMEM_BYTES * 3 // 4),
        name="mixer_router",
    )(x, pool_hist, conv_hist, *weights)


def _tile_rows(index):
    return pl.ds(pl.multiple_of(index * ROW_TILE, ROW_TILE), ROW_TILE)


def _sum_tiles(p):
    sub = lax.broadcasted_iota(jnp.int32, (V7X_SUBLANES, V7X_LANES), 0)
    m4 = sub < 4
    m2 = (sub & 2) == 0
    m1 = (sub & 1) == 0
    b = [jnp.where(m4, p[i], p[i + 4]) + pltpu.roll(jnp.where(m4, p[i + 4], p[i]), 4, 0)
         for i in range(4)]
    c = [jnp.where(m2, b[i], pltpu.roll(b[i + 2], 2, 0))
         + jnp.where(m2, pltpu.roll(b[i], 6, 0), b[i + 2]) for i in range(2)]
    return (jnp.where(m1, c[0], pltpu.roll(c[1], 1, 0))
            + jnp.where(m1, pltpu.roll(c[0], 7, 0), c[1]))


def _tree_sum(terms):
    while len(terms) > 1:
        terms = [terms[i] + terms[i + 1] for i in range(0, len(terms), 2)]
    return terms[0]


def _peer_experts_kernel(meta_ref, gate_ref, h2_ref, acc_ref, uv_ref, out_ref,
                         pair_scr, gate_scr, v_scr, group_tab):
    chunk = pl.program_id(0)

    @pl.when((chunk == 0) & (pl.program_id(1) == 0))
    def _():
        pair_scr[...] = jnp.zeros(pair_scr.shape, F32)
        gate_scr[...] = jnp.zeros(gate_scr.shape, F32)

    sub = lax.broadcasted_iota(jnp.int32, (V7X_SUBLANES, V7X_LANES), 0)
    lane = lax.broadcasted_iota(jnp.int32, (V7X_SUBLANES, V7X_LANES), 1)
    lane_minus_sub = [lane - sub - half * V7X_SUBLANES for half in range(GROUP // V7X_SUBLANES)]

    def list_token(t, count):
        bounds = t * META_ROWS + GSTART_ROW + chunk
        first = meta_ref[bounds]
        mine = meta_ref[bounds + 1] - first
        first_entry = t * META_ROWS + first * GROUP
        for run in range(MAX_TOKEN_GROUPS // TABLE_RUN):
            def write_run(run=run):
                for j in range(run * TABLE_RUN, (run + 1) * TABLE_RUN):
                    group_tab[count + j] = first_entry + j * GROUP
            if run == 0:
                write_run()
            else:
                pl.when(mine > run * TABLE_RUN)(write_run)
        return count + mine

    n_groups = lax.fori_loop(0, PEER_BLOCK, list_token, 0)
    for q in range(GROUPS_PER_STEP - 1):
        group_tab[n_groups + q] = ZERO_ENTRY
    n_steps = (n_groups + GROUPS_PER_STEP - 1) // GROUPS_PER_STEP

    def token_of(entry):
        return lax.shift_right_logical(entry, META_ROWS.bit_length() - 1)

    out_ref[...] = acc_ref[...]

    def batch(b, carry):
        step0 = b * BATCH_STEPS
        step1 = jnp.minimum(step0 + BATCH_STEPS, n_steps)

        def dots(i, carry):
            for q in range(GROUPS_PER_STEP):
                entry = group_tab[i * GROUPS_PER_STEP + q]
                slot = (i - step0) * GROUPS_PER_STEP + q
                h = h2_ref[_tile_rows(token_of(entry)), :]
                gates = gate_ref[pl.ds(lax.shift_right_logical(entry, 7), 1), :]
                first_lane = entry & (V7X_LANES - 1)
                for half in range(GROUP // V7X_SUBLANES):
                    prods = []
                    for j in range(V7X_SUBLANES):
                        k = half * V7X_SUBLANES + j
                        off = meta_ref[entry + k]
                        uv = uv_ref[pl.ds(pl.multiple_of(off, 2 * ROW_TILE), 2 * ROW_TILE), :]
                        prods.append(uv[:ROW_TILE, :] * h)
                        v_scr[_tile_rows(slot * GROUP + k), :] = uv[ROW_TILE:, :]
                    pair_scr[_tile_rows(slot * 2 + half), :] = _sum_tiles(prods)
                    own = lane_minus_sub[half] == first_lane
                    gate_scr[_tile_rows(slot * 2 + half), :] = jnp.where(own, gates, 0.0)
            return carry

        lax.fori_loop(step0, step1, dots, 0)

        def activate(i, carry):
            rows = pl.ds(pl.multiple_of(i * SLAB_ROWS, SLAB_ROWS), SLAB_ROWS)
            act = jnp.sum(pair_scr[rows, :], axis=-1, keepdims=True)
            gate = jnp.sum(gate_scr[rows, :], axis=-1, keepdims=True)
            pair_scr[rows, :] = jnp.broadcast_to(gate * _gelu_tanh(act), (SLAB_ROWS, V7X_LANES))
            return carry

        n_rows = (step1 - step0) * (GROUPS_PER_STEP * GROUP)
        lax.fori_loop(0, (n_rows + SLAB_ROWS - 1) // SLAB_ROWS, activate, 0)

        def mix(i, carry):
            for q in range(GROUPS_PER_STEP):
                entry = group_tab[i * GROUPS_PER_STEP + q]
                slot = (i - step0) * GROUPS_PER_STEP + q
                terms = []
                for half in range(GROUP // V7X_SUBLANES):
                    coef = pair_scr[_tile_rows(slot * 2 + half), :]
                    for j in range(V7X_SUBLANES):
                        k = half * V7X_SUBLANES + j
                        cj = jnp.broadcast_to(coef[j:j + 1, :], (V7X_SUBLANES, V7X_LANES))
                        terms.append(cj * v_scr[_tile_rows(slot * GROUP + k), :])
                rows = _tile_rows(token_of(entry))
                out_ref[rows, :] = out_ref[rows, :] + _tree_sum(terms)
            return carry

        lax.fori_loop(step0, step1, mix, 0)
        return carry

    lax.fori_loop(0, (n_steps + BATCH_STEPS - 1) // BATCH_STEPS, batch, 0)


def _peer_experts(meta, pgate, h2_tiles, peer_uv):
    n_tok = meta.shape[0] // META_ROWS
    tok_spec = pl.BlockSpec((PEER_BLOCK * ROW_TILE, V7X_LANES), lambda c, i: (i, 0))
    table_spec = pl.BlockSpec((CHUNK_EXPERTS * 2 * ROW_TILE, V7X_LANES), lambda c, i: (c, 0),
                              pipeline_mode=pl.Buffered(1))
    max_groups = (PEER_BLOCK + 1) * MAX_TOKEN_GROUPS + GROUPS_PER_STEP
    batch_groups = BATCH_STEPS * GROUPS_PER_STEP
    pair_rows = batch_groups * GROUP
    assert pair_rows % SLAB_ROWS == 0
    v_rows = pair_rows * ROW_TILE
    table_bytes = 2 * CHUNK_EXPERTS * D_MODEL * 4
    block_bytes = PEER_BLOCK * D_MODEL * 4
    vmem_bytes = (table_bytes + (2 * pair_rows + v_rows) * V7X_LANES * 4 + 7 * block_bytes
                  + (4 << 20))
    assert vmem_bytes <= V7X_VMEM_BYTES
    gate_spec = pl.BlockSpec((PEER_BLOCK * META_ROWS // V7X_LANES, V7X_LANES), lambda c, i: (i, 0))
    return pl.pallas_call(
        _peer_experts_kernel,
        grid=(PEER_CHUNKS, n_tok // PEER_BLOCK),
        in_specs=[
            pl.BlockSpec((PEER_BLOCK * META_ROWS,), lambda c, i: (i,), memory_space=pltpu.SMEM),
            gate_spec, tok_spec, tok_spec, table_spec,
        ],
        out_specs=tok_spec,
        out_shape=jax.ShapeDtypeStruct((n_tok * ROW_TILE, V7X_LANES), F32),
        scratch_shapes=[pltpu.VMEM((pair_rows, V7X_LANES), F32),
                        pltpu.VMEM((pair_rows, V7X_LANES), F32),
                        pltpu.VMEM((v_rows, V7X_LANES), F32),
                        pltpu.SMEM((max_groups,), jnp.int32)],
        input_output_aliases={3: 0},
        compiler_params=pltpu.CompilerParams(
            dimension_semantics=("arbitrary", "arbitrary"),
            vmem_limit_bytes=vmem_bytes),
        name="peer_experts",
    )(meta, pgate, h2_tiles, jnp.zeros((n_tok * ROW_TILE, V7X_LANES), F32), peer_uv)


def _final_norm_kernel(x1_ref, peer_ref, g_ref, y_ref):
    peer = jnp.concatenate(
        [peer_ref[pl.ds(j, NORM_BLOCK, stride=ROW_TILE), :] for j in range(ROW_TILE)], axis=-1)
    y_ref[...] = _rms(x1_ref[...] + peer, g_ref[...])


def _final_norm(x1, peer_tiles, g_final):
    n_tok = x1.shape[0]
    spec = pl.BlockSpec((NORM_BLOCK, D_MODEL), lambda i: (i, 0))
    tile_spec = pl.BlockSpec((NORM_BLOCK * ROW_TILE, V7X_LANES), lambda i: (i, 0))
    return pl.pallas_call(
        _final_norm_kernel,
        grid=(n_tok // NORM_BLOCK,),
        in_specs=[spec, tile_spec, pl.BlockSpec((1, D_MODEL), lambda i: (0, 0))],
        out_specs=spec,
        out_shape=jax.ShapeDtypeStruct((n_tok, D_MODEL), F32),
        name="final_norm",
    )(x1, peer_tiles, g_final)


def _trunk(x, pool_hist, conv_hist, pos0, seqs, steps, weights, g_final, uv_tiles):
    nseq, length, _ = x.shape
    n_tok = nseq * length
    x1, h2_tiles, meta_t, pgate_t, new_pool, new_conv = _mixer_router(
        x, pool_hist, conv_hist, pos0, seqs, steps, weights)
    peer_tiles = _peer_experts(meta_t.T.reshape(-1), pgate_t.T.reshape(-1, V7X_LANES),
                               h2_tiles, uv_tiles)
    y = _final_norm(x1.reshape(n_tok, D_MODEL), peer_tiles, g_final)
    return y.reshape(nseq, length, D_MODEL), new_pool[None], new_conv[None]


def kernel(x_prompt, x_sample, state_pool, state_conv, g_mix, w_in, b_in, w_pool, b_pool,
           pool_scale, w_dw, b_dw, ln_g, ln_b, w_out, g_ffn, w_query, sub_keys, peer_u,
           peer_v, g_final):
    assert g_mix.shape[0] == 1, "single-layer trunk"
    wq_hi, wq_lo = _split_bf16(w_query[0])
    keys = sub_keys[0].reshape(2 * PEER_HEADS, PEER_NK, PEER_DKH)
    keys_hi, keys_lo = _split_bf16(keys)
    row = lambda v: v.reshape(1, -1)
    weights = (
        row(g_mix[0]), w_in[0].astype(BF16), row(b_in[0]), w_pool[0].astype(BF16),
        row(b_pool[0]), row(pool_scale[0]), w_dw[0], row(b_dw[0]), row(ln_g[0]),
        row(ln_b[0]), w_out[0].astype(BF16), row(g_ffn[0]), wq_hi, wq_lo, keys_hi, keys_lo)
    g_final_row = row(g_final)
    uv_tiles = jnp.stack(
        [peer_u[0].reshape(PEER_N, ROW_TILE, V7X_LANES),
         peer_v[0].reshape(PEER_N, ROW_TILE, V7X_LANES)], axis=1,
    ).reshape(PEER_N * 2 * ROW_TILE, V7X_LANES)

    n_prompt = x_prompt.shape[0]
    zero_pool = jnp.zeros((n_prompt, POOL_HIST, POOL_W), F32)
    zero_conv = jnp.zeros((n_prompt, CONV_HIST, CONV_W), F32)
    y_p, pool_p, conv_p = _trunk(
        x_prompt, zero_pool, zero_conv, 0, 1, BLOCK_TOKENS, weights, g_final_row, uv_tiles)
    dec_seq = x_sample.shape[1]
    y_s, pool_s, conv_s = _trunk(
        x_sample, state_pool[0], state_conv[0], PAST_LEN, BLOCK_TOKENS // dec_seq, dec_seq,
        weights, g_final_row, uv_tiles)
    return (y_p, y_s, pool_p, conv_p, pool_s, conv_s)
```

```python
import functools
import math

import jax
import jax.numpy as jnp
from jax import lax
from jax.experimental import pallas as pl
from jax.experimental.pallas import tpu as pltpu

D_MODEL = 1024
POOL_W = 512
POOL_WINDOWS = (2, 4, 8, 16)
POOL_GW = POOL_W // len(POOL_WINDOWS)
POOL_HIST = max(POOL_WINDOWS) - 1
CONV_W = 512
CONV_K = 31
CONV_HIST = CONV_K - 1
IN_W = POOL_W + 2 * CONV_W
PEER_HEADS = 8
PEER_NK = 128
PEER_N = PEER_NK * PEER_NK
PEER_DKH = 128
PEER_TOPK = 16
PEER_SLOTS = PEER_HEADS * PEER_TOPK
PAST_LEN = 1024
EPS = 1e-6

V7X_LANES = 128
V7X_SUBLANES = 8
V7X_VMEM_BYTES = 64 * 1024 * 1024

POOL_HALO = 16
CONV_HALO = 32

BLOCK_TOKENS = 256
PEER_BLOCK = 128
NORM_BLOCK = 512

ROW_TILE = D_MODEL // V7X_LANES
PEER_CHUNKS = 4
CHUNK_EXPERTS = PEER_N // PEER_CHUNKS
GROUP = 2 * V7X_SUBLANES
LIST_ROWS = 192
META_ROWS = 256
GSTART_ROW = LIST_ROWS
SLAB_ROWS = 512
GROUPS_PER_STEP = 4
BATCH_STEPS = 8
MAX_TOKEN_GROUPS = PEER_SLOTS // GROUP
TABLE_RUN = 4
ZERO_ENTRY = GSTART_ROW + GROUP
assert ZERO_ENTRY >= GSTART_ROW + V7X_SUBLANES and ZERO_ENTRY + GROUP <= META_ROWS
assert META_ROWS & (META_ROWS - 1) == 0
assert LIST_ROWS >= PEER_SLOTS + PEER_CHUNKS * (GROUP - 1)
COUNT_BITS = 8
assert PEER_SLOTS < (1 << COUNT_BITS) and PEER_CHUNKS * COUNT_BITS <= 32
assert ROW_TILE == V7X_SUBLANES

F32 = jnp.float32
BF16 = jnp.bfloat16


def _rms(x, g):
    return x * lax.rsqrt(jnp.mean(x * x, axis=-1, keepdims=True) + EPS) * g


def _sigmoid(x):
    return 1.0 / (1.0 + jnp.exp(-x))


def _gelu_tanh(x):
    c = math.sqrt(2.0 / math.pi)
    return 0.5 * x * (1.0 + jnp.tanh(c * (x + 0.044715 * (x * x * x))))


def _split_bf16(a):
    hi = a.astype(BF16)
    lo = (a - hi.astype(F32)).astype(BF16)
    return hi, lo


def _dot(a, b):
    return jnp.dot(a, b, preferred_element_type=F32)


def _dot_nt(a, b):
    return lax.dot_general(a, b, (((1,), (1,)), ((), ())), preferred_element_type=F32)


def _row_iota(shape):
    return lax.broadcasted_iota(jnp.int32, shape, 0).astype(F32)


def _topk_rows(s, k, ids=None):
    n, width = s.shape
    if ids is None:
        ids = _row_iota((n, width))
    out_rows = lax.broadcasted_iota(jnp.int32, (k, width), 0)
    vals = jnp.zeros((k, width), F32)
    idxs = jnp.zeros((k, width), F32)
    for j in range(k):
        m = jnp.max(s, axis=0, keepdims=True)
        i = jnp.min(jnp.where(s == m, ids, jnp.inf), axis=0, keepdims=True)
        vals = jnp.where(out_rows == j, m, vals)
        idxs = jnp.where(out_rows == j, i, idxs)
        s = jnp.where(ids == i, -jnp.inf, s)
    return vals, idxs


def _lookup_rows(table, rows):
    out = jnp.zeros(rows.shape, F32)
    for r in range(table.shape[0]):
        out = jnp.where(rows == float(r), table[r:r + 1, :], out)
    return out


def _pair_candidates(s1, s2):
    width = s1.shape[1]
    sub = lax.broadcasted_iota(jnp.int32, (V7X_SUBLANES, width), 0)
    subf = sub.astype(F32)
    low = sub < 4
    s2_lo, s2_hi = s2[0:8, :], s2[8:16, :]
    s2_lo_twice = jnp.where(low, s2_lo, pltpu.roll(s2_lo, 4, 0))
    b_twice = jnp.where(low, subf, subf - 4.0)
    sums = [s1[0:1, :] + s2_lo, s1[0:1, :] + s2_hi]
    ids = [subf, subf + 8.0]
    for a in (1, 2, 3):
        sums.append(s1[a:a + 1, :] + s2_lo)
        ids.append(subf + 16.0 * a)
    for a in (4, 6):
        sums.append(jnp.where(low, s1[a:a + 1, :], s1[a + 1:a + 2, :]) + s2_lo_twice)
        ids.append(jnp.where(low, 16.0 * a, 16.0 * (a + 1)) + b_twice)
    sums.append(s1[8:16, :] + s2[0:1, :])
    ids.append((subf + 8.0) * 16.0)
    return jnp.concatenate(sums, axis=0), jnp.concatenate(ids, axis=0)


def _mixer_router_kernel(
        x_ref, ph_ref, ch_ref, g_mix_ref, w_in_ref, b_in_ref, w_pool_ref, b_pool_ref,
        pool_scale_ref, w_dw_ref, b_dw_ref, ln_g_ref, ln_b_ref, w_out_ref, g_ffn_ref,
        wq_hi_ref, wq_lo_ref, keys_hi_ref, keys_lo_ref,
        x1_ref, h2_ref, meta_ref, pgate_ref, new_pool_ref, new_conv_ref,
        zbuf, ubuf, s_scr, top_s, top_i, e_scr, g_scr, p_scr, off_scr,
        *, pos0, seqs, steps):
    tb = pl.program_id(1)
    tokens = seqs * steps

    @pl.when(tb == 0)
    def _():
        zbuf[:, POOL_HALO - POOL_HIST:POOL_HALO, :] = ph_ref[...]
        ubuf[:, CONV_HALO - CONV_HIST:CONV_HALO, :] = ch_ref[...]

    x = x_ref[...].reshape(tokens, D_MODEL)
    h = _rms(x, g_mix_ref[...])
    z = _dot(h.astype(BF16), w_in_ref[...]) + b_in_ref[...]
    za = z[:, :POOL_W]
    zv = z[:, POOL_W:POOL_W + CONV_W]
    zg = z[:, POOL_W + CONV_W:]
    u = zv * _sigmoid(zg)
    zbuf[:, POOL_HALO:, :] = za.reshape(seqs, steps, POOL_W)
    ubuf[:, CONV_HALO:, :] = u.reshape(seqs, steps, CONV_W)

    pos = pos0 + tb * steps + lax.broadcasted_iota(jnp.int32, (steps, 1), 0)

    a_parts, c_parts = [], []
    for s in range(seqs):
        groups = []
        for g, w in enumerate(POOL_WINDOWS):
            cols = slice(g * POOL_GW, (g + 1) * POOL_GW)
            tot = zbuf[s, POOL_HALO:POOL_HALO + steps, cols]
            for j in range(1, w):
                tot = tot + zbuf[s, POOL_HALO - j:POOL_HALO - j + steps, cols]
            cnt = jnp.minimum(pos + 1, w).astype(F32)
            d = tot / cnt - zbuf[s, POOL_HALO:POOL_HALO + steps, cols]
            groups.append(_dot(d.astype(BF16), w_pool_ref[g]))
        a_parts.append(jnp.concatenate(groups, axis=-1))

        conv = jnp.zeros((steps, CONV_W), F32)
        for k in range(CONV_K):
            lo = CONV_HALO - CONV_HIST + k
            conv = conv + ubuf[s, lo:lo + steps, :] * w_dw_ref[k:k + 1, :]
        c_parts.append(conv)

    a = jnp.concatenate(a_parts, axis=0) if seqs > 1 else a_parts[0]
    a = (a + b_pool_ref[...]) * pool_scale_ref[...]
    c = jnp.concatenate(c_parts, axis=0) if seqs > 1 else c_parts[0]
    c = c + b_dw_ref[...]
    mu = jnp.mean(c, axis=-1, keepdims=True)
    var = jnp.mean(jnp.square(c - mu), axis=-1, keepdims=True)
    c = (c - mu) * lax.rsqrt(var + EPS) * ln_g_ref[...] + ln_b_ref[...]
    c = c * _sigmoid(c)

    x1 = (x + _dot(a.astype(BF16), w_out_ref[:POOL_W, :])
          + _dot(c.astype(BF16), w_out_ref[POOL_W:, :]))
    x1_ref[...] = x1.reshape(seqs, steps, D_MODEL)

    new_pool_ref[...] = zbuf[:, steps + POOL_HALO - POOL_HIST:steps + POOL_HALO, :]
    new_conv_ref[...] = ubuf[:, steps + CONV_HALO - CONV_HIST:steps + CONV_HALO, :]
    zbuf[:, POOL_HALO - POOL_HIST:POOL_HALO, :] = (
        zbuf[:, steps + POOL_HALO - POOL_HIST:steps + POOL_HALO, :])
    ubuf[:, CONV_HALO - CONV_HIST:CONV_HALO, :] = (
        ubuf[:, steps + CONV_HALO - CONV_HIST:steps + CONV_HALO, :])

    h2 = _rms(x1, g_ffn_ref[...])
    for j in range(ROW_TILE):
        h2_ref[pl.ds(j, tokens, stride=ROW_TILE), :] = h2[:, j * V7X_LANES:(j + 1) * V7X_LANES]
    h2_hi, h2_lo = _split_bf16(h2)
    q = (_dot(h2_hi, wq_hi_ref[...]) + _dot(h2_lo, wq_hi_ref[...])
         + _dot(h2_hi, wq_lo_ref[...]))
    for l in range(2 * PEER_HEADS):
        q_hi, q_lo = _split_bf16(q[:, l * PEER_DKH:(l + 1) * PEER_DKH])
        k_hi = keys_hi_ref[l]
        k_lo = keys_lo_ref[l]
        s_scr[l] = _dot_nt(k_hi, q_hi) + _dot_nt(k_hi, q_lo) + _dot_nt(k_lo, q_hi)

    lane_groups = tokens // V7X_LANES

    def stage1(hd, carry):
        for l in (2 * hd, 2 * hd + 1):
            for lg in range(lane_groups):
                lanes = slice(lg * V7X_LANES, (lg + 1) * V7X_LANES)
                vals, idxs = _topk_rows(s_scr[l, :, lanes], PEER_TOPK)
                top_s[l, :, lanes] = vals
                top_i[l, :, lanes] = idxs
        return carry

    lax.fori_loop(0, PEER_HEADS, stage1, 0)

    def stage2(hd, carry):
        rows = pl.ds(pl.multiple_of(hd * PEER_TOPK, PEER_TOPK), PEER_TOPK)
        for lg in range(lane_groups):
            lanes = slice(lg * V7X_LANES, (lg + 1) * V7X_LANES)
            comb, flat = _pair_candidates(top_s[2 * hd, :, lanes], top_s[2 * hd + 1, :, lanes])
            f_s, f_i = _topk_rows(comb, PEER_TOPK, flat)
            ra = jnp.floor(f_i * (1.0 / PEER_TOPK))
            rb = f_i - ra * PEER_TOPK
            eidx = (_lookup_rows(top_i[2 * hd, :, lanes], ra) * PEER_NK
                    + _lookup_rows(top_i[2 * hd + 1, :, lanes], rb))
            p = jnp.exp(f_s - jnp.max(f_s, axis=0, keepdims=True))
            e_scr[rows, lanes] = eidx
            g_scr[rows, lanes] = p / jnp.sum(p, axis=0, keepdims=True)
        return carry

    lax.fori_loop(0, PEER_HEADS, stage2, 0)

    def stage3(lg):
        lanes = slice(lg * V7X_LANES, (lg + 1) * V7X_LANES)
        e = e_scr[:, lanes]
        chunk = jnp.floor(e * (1.0 / CHUNK_EXPERTS))

        def tile_of(ref, kb):
            return ref[pl.ds(pl.multiple_of(kb * V7X_SUBLANES, V7X_SUBLANES), V7X_SUBLANES), lanes]

        field = chunk.astype(jnp.int32) * COUNT_BITS
        ones = jnp.left_shift(jnp.ones_like(field), field)
        slot = lax.broadcasted_iota(jnp.int32, (PEER_SLOTS, V7X_LANES), 0)
        run = ones
        step = 1
        while step < PEER_SLOTS:
            run = run + jnp.where(slot >= step, pltpu.roll(run, step, 0), 0)
            step *= 2
        count_mask = (1 << COUNT_BITS) - 1
        before = (lax.shift_right_logical(run - ones, field) & count_mask).astype(F32)
        totals = run[PEER_SLOTS - 1:PEER_SLOTS, :]
        pstart = jnp.zeros((1, V7X_LANES), F32)
        shift = jnp.zeros_like(e)
        grow = lax.broadcasted_iota(jnp.int32, (V7X_SUBLANES, V7X_LANES), 0)
        gstart = jnp.zeros((V7X_SUBLANES, V7X_LANES), F32)
        for ch in range(PEER_CHUNKS):
            n = (lax.shift_right_logical(totals, ch * COUNT_BITS) & count_mask).astype(F32)
            shift = jnp.where(chunk == float(ch), pstart, shift)
            pstart = pstart + jnp.ceil(n * (1.0 / GROUP)) * GROUP
            gstart = jnp.where(grow == ch + 1, pstart * (1.0 / GROUP), gstart)
        p_scr[:, lanes] = before + shift
        off_scr[:, lanes] = (e - chunk * CHUNK_EXPERTS) * (2 * ROW_TILE)

        list_rows = _row_iota((LIST_ROWS, V7X_LANES))

        def place(kb, lists):
            offs, gates = lists
            p8, off8, gate8 = tile_of(p_scr, kb), tile_of(off_scr, kb), tile_of(g_scr, kb)
            for j in range(V7X_SUBLANES):
                hit = list_rows == p8[j:j + 1, :]
                offs = jnp.where(hit, off8[j:j + 1, :], offs)
                gates = jnp.where(hit, gate8[j:j + 1, :], gates)
            return offs, gates

        zeros = jnp.zeros((LIST_ROWS, V7X_LANES), F32)
        offs, gates = lax.fori_loop(0, PEER_SLOTS // V7X_SUBLANES, place, (zeros, zeros))
        meta_ref[0:LIST_ROWS, lanes] = offs.astype(jnp.int32)
        meta_ref[GSTART_ROW:GSTART_ROW + V7X_SUBLANES, lanes] = gstart.astype(jnp.int32)
        meta_ref[GSTART_ROW + V7X_SUBLANES:, lanes] = jnp.zeros(
            (META_ROWS - GSTART_ROW - V7X_SUBLANES, V7X_LANES), jnp.int32)
        pgate_ref[0:LIST_ROWS, lanes] = gates
        pgate_ref[LIST_ROWS:, lanes] = jnp.zeros((META_ROWS - LIST_ROWS, V7X_LANES), F32)

    for lg in range(lane_groups):
        stage3(lg)


def _const_spec(shape):
    zeros = (0,) * len(shape)
    return pl.BlockSpec(shape, lambda i, j: zeros, pipeline_mode=pl.Buffered(1))


def _mixer_router(x, pool_hist, conv_hist, pos0, seqs, steps, weights):
    nseq, length, _ = x.shape
    tokens = seqs * steps
    grid = (nseq // seqs, length // steps)
    blocks_per_seq_group = length // steps
    per_time = pl.BlockSpec((seqs, steps, D_MODEL), lambda i, j: (i, j, 0))
    per_token = pl.BlockSpec((tokens * ROW_TILE, V7X_LANES),
                             lambda i, j: (i * blocks_per_seq_group + j, 0))
    per_seq_pool = pl.BlockSpec((seqs, POOL_HIST, POOL_W), lambda i, j: (i, 0, 0))
    per_seq_conv = pl.BlockSpec((seqs, CONV_HIST, CONV_W), lambda i, j: (i, 0, 0))

    def slot_spec(rows):
        return pl.BlockSpec((rows, tokens), lambda i, j: (0, i * blocks_per_seq_group + j))

    n_tok = nseq * length
    kernel = functools.partial(_mixer_router_kernel, pos0=pos0, seqs=seqs, steps=steps)
    slot_scratch = pltpu.VMEM((PEER_SLOTS, tokens), F32)
    return pl.pallas_call(
        kernel,
        grid=grid,
        in_specs=[per_time, per_seq_pool, per_seq_conv] + [_const_spec(w.shape) for w in weights],
        out_specs=[per_time, per_token, slot_spec(META_ROWS), slot_spec(META_ROWS),
                   per_seq_pool, per_seq_conv],
        out_shape=[
            jax.ShapeDtypeStruct((nseq, length, D_MODEL), F32),
            jax.ShapeDtypeStruct((n_tok * ROW_TILE, V7X_LANES), F32),
            jax.ShapeDtypeStruct((META_ROWS, n_tok), jnp.int32),
            jax.ShapeDtypeStruct((META_ROWS, n_tok), F32),
            jax.ShapeDtypeStruct((nseq, POOL_HIST, POOL_W), F32),
            jax.ShapeDtypeStruct((nseq, CONV_HIST, CONV_W), F32),
        ],
        scratch_shapes=[
            pltpu.VMEM((seqs, POOL_HALO + steps, POOL_W), F32),
            pltpu.VMEM((seqs, CONV_HALO + steps, CONV_W), F32),
            pltpu.VMEM((2 * PEER_HEADS, PEER_NK, tokens), F32),
            pltpu.VMEM((2 * PEER_HEADS, PEER_TOPK, tokens), F32),
            pltpu.VMEM((2 * PEER_HEADS, PEER_TOPK, tokens), F32),
            slot_scratch, slot_scratch, slot_scratch, slot_scratch,
        ],
        compiler_params=pltpu.CompilerParams(
            dimension_semantics=("arbitrary", "arbitrary"),
            vmem_limit_bytes=V7X_VMEM_BYTES * 3 // 4),
        name="mixer_router",
    )(x, pool_hist, conv_hist, *weights)


def _tile_rows(index):
    return pl.ds(pl.multiple_of(index * ROW_TILE, ROW_TILE), ROW_TILE)


def _sum_tiles(p):
    sub = lax.broadcasted_iota(jnp.int32, (V7X_SUBLANES, V7X_LANES), 0)
    m4 = sub < 4
    m2 = (sub & 2) == 0
    m1 = (sub & 1) == 0
    b = [jnp.where(m4, p[i], p[i + 4]) + pltpu.roll(jnp.where(m4, p[i + 4], p[i]), 4, 0)
         for i in range(4)]
    c = [jnp.where(m2, b[i], pltpu.roll(b[i + 2], 2, 0))
         + jnp.where(m2, pltpu.roll(b[i], 6, 0), b[i + 2]) for i in range(2)]
    return (jnp.where(m1, c[0], pltpu.roll(c[1], 1, 0))
            + jnp.where(m1, pltpu.roll(c[0], 7, 0), c[1]))


def _tree_sum(terms):
    while len(terms) > 1:
        terms = [terms[i] + terms[i + 1] for i in range(0, len(terms), 2)]
    return terms[0]


def _peer_experts_kernel(meta_ref, gate_ref, h2_ref, acc_ref, uv_ref, out_ref,
                         pair_a, gate_a, v_a, pair_b, gate_b, v_b, group_tab):
    chunk = pl.program_id(0)

    @pl.when((chunk == 0) & (pl.program_id(1) == 0))
    def _():
        for ref in (pair_a, gate_a, pair_b, gate_b):
            ref[...] = jnp.zeros(ref.shape, F32)

    sub = lax.broadcasted_iota(jnp.int32, (V7X_SUBLANES, V7X_LANES), 0)
    lane = lax.broadcasted_iota(jnp.int32, (V7X_SUBLANES, V7X_LANES), 1)
    lane_minus_sub = [lane - sub - half * V7X_SUBLANES for half in range(GROUP // V7X_SUBLANES)]

    def list_token(t, count):
        bounds = t * META_ROWS + GSTART_ROW + chunk
        first = meta_ref[bounds]
        mine = meta_ref[bounds + 1] - first
        first_entry = t * META_ROWS + first * GROUP
        for run in range(MAX_TOKEN_GROUPS // TABLE_RUN):
            def write_run(run=run):
                for j in range(run * TABLE_RUN, (run + 1) * TABLE_RUN):
                    group_tab[count + j] = first_entry + j * GROUP
            if run == 0:
                write_run()
            else:
                pl.when(mine > run * TABLE_RUN)(write_run)
        return count + mine

    n_groups = lax.fori_loop(0, PEER_BLOCK, list_token, 0)
    for q in range(GROUPS_PER_STEP - 1):
        group_tab[n_groups + q] = ZERO_ENTRY
    n_steps = (n_groups + GROUPS_PER_STEP - 1) // GROUPS_PER_STEP

    def token_of(entry):
        return lax.shift_right_logical(entry, META_ROWS.bit_length() - 1)

    out_ref[...] = acc_ref[...]

    buffers = ((pair_a, gate_a, v_a), (pair_b, gate_b, v_b))

    def dots_step(i, step0, bufs):
        pair_scr, gate_scr, v_scr = bufs
        for q in range(GROUPS_PER_STEP):
            entry = group_tab[i * GROUPS_PER_STEP + q]
            slot = (i - step0) * GROUPS_PER_STEP + q
            h = h2_ref[_tile_rows(token_of(entry)), :]
            gates = gate_ref[pl.ds(lax.shift_right_logical(entry, 7), 1), :]
            first_lane = entry & (V7X_LANES - 1)
            for half in range(GROUP // V7X_SUBLANES):
                prods = []
                for j in range(V7X_SUBLANES):
                    k = half * V7X_SUBLANES + j
                    off = meta_ref[entry + k]
                    uv = uv_ref[pl.ds(pl.multiple_of(off, 2 * ROW_TILE), 2 * ROW_TILE), :]
                    prods.append(uv[:ROW_TILE, :] * h)
                    v_scr[_tile_rows(slot * GROUP + k), :] = uv[ROW_TILE:, :]
                pair_scr[_tile_rows(slot * 2 + half), :] = _sum_tiles(prods)
                own = lane_minus_sub[half] == first_lane
                gate_scr[_tile_rows(slot * 2 + half), :] = jnp.where(own, gates, 0.0)

    def activate(n_batch_steps, bufs):
        pair_scr, gate_scr, _ = bufs

        def slab(i, carry):
            rows = pl.ds(pl.multiple_of(i * SLAB_ROWS, SLAB_ROWS), SLAB_ROWS)
            act = jnp.sum(pair_scr[rows, :], axis=-1, keepdims=True)
            gate = jnp.sum(gate_scr[rows, :], axis=-1, keepdims=True)
            pair_scr[rows, :] = jnp.broadcast_to(gate * _gelu_tanh(act), (SLAB_ROWS, V7X_LANES))
            return carry

        n_rows = n_batch_steps * (GROUPS_PER_STEP * GROUP)
        lax.fori_loop(0, (n_rows + SLAB_ROWS - 1) // SLAB_ROWS, slab, 0)

    def mix_step(i, step0, bufs):
        pair_scr, _, v_scr = bufs
        for q in range(GROUPS_PER_STEP):
            entry = group_tab[i * GROUPS_PER_STEP + q]
            slot = (i - step0) * GROUPS_PER_STEP + q
            terms = []
            for half in range(GROUP // V7X_SUBLANES):
                coef = pair_scr[_tile_rows(slot * 2 + half), :]
                for j in range(V7X_SUBLANES):
                    k = half * V7X_SUBLANES + j
                    cj = jnp.broadcast_to(coef[j:j + 1, :], (V7X_SUBLANES, V7X_LANES))
                    terms.append(cj * v_scr[_tile_rows(slot * GROUP + k), :])
            rows = _tile_rows(token_of(entry))
            out_ref[rows, :] = out_ref[rows, :] + _tree_sum(terms)

    def loop_steps(step0, step1, bufs, step_fn):
        def body(i, carry):
            step_fn(i, step0, bufs)
            return carry
        lax.fori_loop(step0, step1, body, 0)

    n_full = n_steps // BATCH_STEPS

    def overlapped(b, mine, previous):
        activate(BATCH_STEPS, previous)

        def body(i, carry):
            dots_step(b * BATCH_STEPS + i, b * BATCH_STEPS, mine)
            mix_step((b - 1) * BATCH_STEPS + i, (b - 1) * BATCH_STEPS, previous)
            return carry

        lax.fori_loop(0, BATCH_STEPS, body, 0)

    @pl.when(n_full > 0)
    def _():
        loop_steps(0, BATCH_STEPS, buffers[0], dots_step)

    def batch_pair(bb, carry):
        b = 2 * bb + 1
        overlapped(b, buffers[1], buffers[0])

        @pl.when(b + 1 < n_full)
        def _():
            overlapped(b + 1, buffers[0], buffers[1])

        return carry

    lax.fori_loop(0, n_full // 2, batch_pair, 0)

    for parity in range(2):
        @pl.when((n_full > 0) & ((n_full - 1) % 2 == parity))
        def _(parity=parity):
            last0 = (n_full - 1) * BATCH_STEPS
            activate(BATCH_STEPS, buffers[parity])
            loop_steps(last0, last0 + BATCH_STEPS, buffers[parity], mix_step)

    tail0 = n_full * BATCH_STEPS

    @pl.when(tail0 < n_steps)
    def _():
        loop_steps(tail0, n_steps, buffers[0], dots_step)
        activate(n_steps - tail0, buffers[0])
        loop_steps(tail0, n_steps, buffers[0], mix_step)


def _peer_experts(meta, pgate, h2_tiles, peer_uv):
    n_tok = meta.shape[0] // META_ROWS
    tok_spec = pl.BlockSpec((PEER_BLOCK * ROW_TILE, V7X_LANES), lambda c, i: (i, 0))
    table_spec = pl.BlockSpec((CHUNK_EXPERTS * 2 * ROW_TILE, V7X_LANES), lambda c, i: (c, 0),
                              pipeline_mode=pl.Buffered(1))
    max_groups = (PEER_BLOCK + 1) * MAX_TOKEN_GROUPS + GROUPS_PER_STEP
    batch_groups = BATCH_STEPS * GROUPS_PER_STEP
    pair_rows = batch_groups * GROUP
    assert pair_rows % SLAB_ROWS == 0
    v_rows = pair_rows * ROW_TILE
    table_bytes = 2 * CHUNK_EXPERTS * D_MODEL * 4
    block_bytes = PEER_BLOCK * D_MODEL * 4
    buffer_set = [pltpu.VMEM((pair_rows, V7X_LANES), F32),
                  pltpu.VMEM((pair_rows, V7X_LANES), F32),
                  pltpu.VMEM((v_rows, V7X_LANES), F32)]
    vmem_bytes = (table_bytes + 2 * (2 * pair_rows + v_rows) * V7X_LANES * 4 + 7 * block_bytes
                  + (4 << 20))
    assert vmem_bytes <= V7X_VMEM_BYTES
    gate_spec = pl.BlockSpec((PEER_BLOCK * META_ROWS // V7X_LANES, V7X_LANES), lambda c, i: (i, 0))
    return pl.pallas_call(
        _peer_experts_kernel,
        grid=(PEER_CHUNKS, n_tok // PEER_BLOCK),
        in_specs=[
            pl.BlockSpec((PEER_BLOCK * META_ROWS,), lambda c, i: (i,), memory_space=pltpu.SMEM),
            gate_spec, tok_spec, tok_spec, table_spec,
        ],
        out_specs=tok_spec,
        out_shape=jax.ShapeDtypeStruct((n_tok * ROW_TILE, V7X_LANES), F32),
        scratch_shapes=buffer_set + buffer_set + [pltpu.SMEM((max_groups,), jnp.int32)],
        input_output_aliases={3: 0},
        compiler_params=pltpu.CompilerParams(
            dimension_semantics=("arbitrary", "arbitrary"),
            vmem_limit_bytes=vmem_bytes),
        name="peer_experts",
    )(meta, pgate, h2_tiles, jnp.zeros((n_tok * ROW_TILE, V7X_LANES), F32), peer_uv)


def _final_norm_kernel(x1_ref, peer_ref, g_ref, y_ref):
    peer = jnp.concatenate(
        [peer_ref[pl.ds(j, NORM_BLOCK, stride=ROW_TILE), :] for j in range(ROW_TILE)], axis=-1)
    y_ref[...] = _rms(x1_ref[...] + peer, g_ref[...])


def _final_norm(x1, peer_tiles, g_final):
    n_tok = x1.shape[0]
    spec = pl.BlockSpec((NORM_BLOCK, D_MODEL), lambda i: (i, 0))
    tile_spec = pl.BlockSpec((NORM_BLOCK * ROW_TILE, V7X_LANES), lambda i: (i, 0))
    return pl.pallas_call(
        _final_norm_kernel,
        grid=(n_tok // NORM_BLOCK,),
        in_specs=[spec, tile_spec, pl.BlockSpec((1, D_MODEL), lambda i: (0, 0))],
        out_specs=spec,
        out_shape=jax.ShapeDtypeStruct((n_tok, D_MODEL), F32),
        name="final_norm",
    )(x1, peer_tiles, g_final)


def _trunk(x, pool_hist, conv_hist, pos0, seqs, steps, weights, g_final, uv_tiles):
    nseq, length, _ = x.shape
    n_tok = nseq * length
    x1, h2_tiles, meta_t, pgate_t, new_pool, new_conv = _mixer_router(
        x, pool_hist, conv_hist, pos0, seqs, steps, weights)
    peer_tiles = _peer_experts(meta_t.T.reshape(-1), pgate_t.T.reshape(-1, V7X_LANES),
                               h2_tiles, uv_tiles)
    y = _final_norm(x1.reshape(n_tok, D_MODEL), peer_tiles, g_final)
    return y.reshape(nseq, length, D_MODEL), new_pool[None], new_conv[None]


def kernel(x_prompt, x_sample, state_pool, state_conv, g_mix, w_in, b_in, w_pool, b_pool,
           pool_scale, w_dw, b_dw, ln_g, ln_b, w_out, g_ffn, w_query, sub_keys, peer_u,
           peer_v, g_final):
    assert g_mix.shape[0] == 1, "single-layer trunk"
    wq_hi, wq_lo = _split_bf16(w_query[0])
    keys = sub_keys[0].reshape(2 * PEER_HEADS, PEER_NK, PEER_DKH)
    keys_hi, keys_lo = _split_bf16(keys)
    row = lambda v: v.reshape(1, -1)
    weights = (
        row(g_mix[0]), w_in[0].astype(BF16), row(b_in[0]), w_pool[0].astype(BF16),
        row(b_pool[0]), row(pool_scale[0]), w_dw[0], row(b_dw[0]), row(ln_g[0]),
        row(ln_b[0]), w_out[0].astype(BF16), row(g_ffn[0]), wq_hi, wq_lo, keys_hi, keys_lo)
    g_final_row = row(g_final)
    uv_tiles = jnp.stack(
        [peer_u[0].reshape(PEER_N, ROW_TILE, V7X_LANES),
         peer_v[0].reshape(PEER_N, ROW_TILE, V7X_LANES)], axis=1,
    ).reshape(PEER_N * 2 * ROW_TILE, V7X_LANES)

    n_prompt = x_prompt.shape[0]
    zero_pool = jnp.zeros((n_prompt, POOL_HIST, POOL_W), F32)
    zero_conv = jnp.zeros((n_prompt, CONV_HIST, CONV_W), F32)
    y_p, pool_p, conv_p = _trunk(
        x_prompt, zero_pool, zero_conv, 0, 1, BLOCK_TOKENS, weights, g_final_row, uv_tiles)
    dec_seq = x_sample.shape[1]
    y_s, pool_s, conv_s = _trunk(
        x_sample, state_pool[0], state_conv[0], PAST_LEN, BLOCK_TOKENS // dec_seq, dec_seq,
        weights, g_final_row, uv_tiles)
    return (y_p, y_s, pool_p, conv_p, pool_s, conv_s)
```

```python
import functools
import math

import jax
import jax.numpy as jnp
from jax import lax
from jax.experimental import pallas as pl
from jax.experimental.pallas import tpu as pltpu

D_MODEL = 1024
POOL_W = 512
POOL_WINDOWS = (2, 4, 8, 16)
POOL_GW = POOL_W // len(POOL_WINDOWS)
POOL_HIST = max(POOL_WINDOWS) - 1
CONV_W = 512
CONV_K = 31
CONV_HIST = CONV_K - 1
IN_W = POOL_W + 2 * CONV_W
PEER_HEADS = 8
PEER_NK = 128
PEER_N = PEER_NK * PEER_NK
PEER_DKH = 128
PEER_TOPK = 16
PEER_SLOTS = PEER_HEADS * PEER_TOPK
PAST_LEN = 1024
EPS = 1e-6

V7X_LANES = 128
V7X_SUBLANES = 8
V7X_VMEM_BYTES = 64 * 1024 * 1024

POOL_HALO = 16
CONV_HALO = 32

BLOCK_TOKENS = 256
PEER_BLOCK = 128
NORM_BLOCK = 512

ROW_TILE = D_MODEL // V7X_LANES
PEER_CHUNKS = 4
CHUNK_EXPERTS = PEER_N // PEER_CHUNKS
GROUP = 2 * V7X_SUBLANES
LIST_ROWS = 192
META_ROWS = 256
GSTART_ROW = LIST_ROWS
SLAB_ROWS = 512
GROUPS_PER_STEP = 4
BATCH_STEPS = 8
TABLE_SLOTS = 1152
TABLE_STRIDE = 2048
assert TABLE_SLOTS % V7X_LANES == 0 and TABLE_STRIDE >= TABLE_SLOTS
assert TABLE_SLOTS > PEER_BLOCK * (PEER_SLOTS // GROUP) + GROUPS_PER_STEP
ZERO_ENTRY = GSTART_ROW + GROUP
assert ZERO_ENTRY >= GSTART_ROW + V7X_SUBLANES and ZERO_ENTRY + GROUP <= META_ROWS
assert META_ROWS & (META_ROWS - 1) == 0
assert LIST_ROWS >= PEER_SLOTS + PEER_CHUNKS * (GROUP - 1)
COUNT_BITS = 8
assert PEER_SLOTS < (1 << COUNT_BITS) and PEER_CHUNKS * COUNT_BITS <= 32
assert ROW_TILE == V7X_SUBLANES

F32 = jnp.float32
BF16 = jnp.bfloat16


def _rms(x, g):
    return x * lax.rsqrt(jnp.mean(x * x, axis=-1, keepdims=True) + EPS) * g


def _sigmoid(x):
    return 1.0 / (1.0 + jnp.exp(-x))


def _gelu_tanh(x):
    c = math.sqrt(2.0 / math.pi)
    return 0.5 * x * (1.0 + jnp.tanh(c * (x + 0.044715 * (x * x * x))))


def _split_bf16(a):
    hi = a.astype(BF16)
    lo = (a - hi.astype(F32)).astype(BF16)
    return hi, lo


def _dot(a, b):
    return jnp.dot(a, b, preferred_element_type=F32)


def _dot_nt(a, b):
    return lax.dot_general(a, b, (((1,), (1,)), ((), ())), preferred_element_type=F32)


def _row_iota(shape):
    return lax.broadcasted_iota(jnp.int32, shape, 0).astype(F32)


def _topk_rows(s, k, ids=None):
    n, width = s.shape
    if ids is None:
        ids = _row_iota((n, width))
    out_rows = lax.broadcasted_iota(jnp.int32, (k, width), 0)
    vals = jnp.zeros((k, width), F32)
    idxs = jnp.zeros((k, width), F32)
    for j in range(k):
        m = jnp.max(s, axis=0, keepdims=True)
        i = jnp.min(jnp.where(s == m, ids, jnp.inf), axis=0, keepdims=True)
        vals = jnp.where(out_rows == j, m, vals)
        idxs = jnp.where(out_rows == j, i, idxs)
        s = jnp.where(ids == i, -jnp.inf, s)
    return vals, idxs


def _lookup_rows(table, rows):
    out = jnp.zeros(rows.shape, F32)
    for r in range(table.shape[0]):
        out = jnp.where(rows == float(r), table[r:r + 1, :], out)
    return out


def _pair_candidates(s1, s2):
    width = s1.shape[1]
    sub = lax.broadcasted_iota(jnp.int32, (V7X_SUBLANES, width), 0)
    subf = sub.astype(F32)
    low = sub < 4
    s2_lo, s2_hi = s2[0:8, :], s2[8:16, :]
    s2_lo_twice = jnp.where(low, s2_lo, pltpu.roll(s2_lo, 4, 0))
    b_twice = jnp.where(low, subf, subf - 4.0)
    sums = [s1[0:1, :] + s2_lo, s1[0:1, :] + s2_hi]
    ids = [subf, subf + 8.0]
    for a in (1, 2, 3):
        sums.append(s1[a:a + 1, :] + s2_lo)
        ids.append(subf + 16.0 * a)
    for a in (4, 6):
        sums.append(jnp.where(low, s1[a:a + 1, :], s1[a + 1:a + 2, :]) + s2_lo_twice)
        ids.append(jnp.where(low, 16.0 * a, 16.0 * (a + 1)) + b_twice)
    sums.append(s1[8:16, :] + s2[0:1, :])
    ids.append((subf + 8.0) * 16.0)
    return jnp.concatenate(sums, axis=0), jnp.concatenate(ids, axis=0)


def _mixer_router_kernel(
        x_ref, ph_ref, ch_ref, g_mix_ref, w_in_ref, b_in_ref, w_pool_ref, b_pool_ref,
        pool_scale_ref, w_dw_ref, b_dw_ref, ln_g_ref, ln_b_ref, w_out_ref, g_ffn_ref,
        wq_hi_ref, wq_lo_ref, keys_hi_ref, keys_lo_ref,
        x1_ref, h2_ref, meta_ref, pgate_ref, tab_ref, new_pool_ref, new_conv_ref,
        zbuf, ubuf, s_scr, top_s, top_i, e_scr, g_scr, p_scr, off_scr,
        *, pos0, seqs, steps):
    tb = pl.program_id(1)
    tokens = seqs * steps

    @pl.when(tb == 0)
    def _():
        zbuf[:, POOL_HALO - POOL_HIST:POOL_HALO, :] = ph_ref[...]
        ubuf[:, CONV_HALO - CONV_HIST:CONV_HALO, :] = ch_ref[...]

    x = x_ref[...].reshape(tokens, D_MODEL)
    h = _rms(x, g_mix_ref[...])
    z = _dot(h.astype(BF16), w_in_ref[...]) + b_in_ref[...]
    za = z[:, :POOL_W]
    zv = z[:, POOL_W:POOL_W + CONV_W]
    zg = z[:, POOL_W + CONV_W:]
    u = zv * _sigmoid(zg)
    zbuf[:, POOL_HALO:, :] = za.reshape(seqs, steps, POOL_W)
    ubuf[:, CONV_HALO:, :] = u.reshape(seqs, steps, CONV_W)

    pos = pos0 + tb * steps + lax.broadcasted_iota(jnp.int32, (steps, 1), 0)

    a_parts, c_parts = [], []
    for s in range(seqs):
        groups = []
        for g, w in enumerate(POOL_WINDOWS):
            cols = slice(g * POOL_GW, (g + 1) * POOL_GW)
            tot = zbuf[s, POOL_HALO:POOL_HALO + steps, cols]
            for j in range(1, w):
                tot = tot + zbuf[s, POOL_HALO - j:POOL_HALO - j + steps, cols]
            cnt = jnp.minimum(pos + 1, w).astype(F32)
            d = tot / cnt - zbuf[s, POOL_HALO:POOL_HALO + steps, cols]
            groups.append(_dot(d.astype(BF16), w_pool_ref[g]))
        a_parts.append(jnp.concatenate(groups, axis=-1))

        conv = jnp.zeros((steps, CONV_W), F32)
        for k in range(CONV_K):
            lo = CONV_HALO - CONV_HIST + k
            conv = conv + ubuf[s, lo:lo + steps, :] * w_dw_ref[k:k + 1, :]
        c_parts.append(conv)

    a = jnp.concatenate(a_parts, axis=0) if seqs > 1 else a_parts[0]
    a = (a + b_pool_ref[...]) * pool_scale_ref[...]
    c = jnp.concatenate(c_parts, axis=0) if seqs > 1 else c_parts[0]
    c = c + b_dw_ref[...]
    mu = jnp.mean(c, axis=-1, keepdims=True)
    var = jnp.mean(jnp.square(c - mu), axis=-1, keepdims=True)
    c = (c - mu) * lax.rsqrt(var + EPS) * ln_g_ref[...] + ln_b_ref[...]
    c = c * _sigmoid(c)

    x1 = (x + _dot(a.astype(BF16), w_out_ref[:POOL_W, :])
          + _dot(c.astype(BF16), w_out_ref[POOL_W:, :]))
    x1_ref[...] = x1.reshape(seqs, steps, D_MODEL)

    new_pool_ref[...] = zbuf[:, steps + POOL_HALO - POOL_HIST:steps + POOL_HALO, :]
    new_conv_ref[...] = ubuf[:, steps + CONV_HALO - CONV_HIST:steps + CONV_HALO, :]
    zbuf[:, POOL_HALO - POOL_HIST:POOL_HALO, :] = (
        zbuf[:, steps + POOL_HALO - POOL_HIST:steps + POOL_HALO, :])
    ubuf[:, CONV_HALO - CONV_HIST:CONV_HALO, :] = (
        ubuf[:, steps + CONV_HALO - CONV_HIST:steps + CONV_HALO, :])

    h2 = _rms(x1, g_ffn_ref[...])
    for j in range(ROW_TILE):
        h2_ref[pl.ds(j, tokens, stride=ROW_TILE), :] = h2[:, j * V7X_LANES:(j + 1) * V7X_LANES]
    h2_hi, h2_lo = _split_bf16(h2)
    q = (_dot(h2_hi, wq_hi_ref[...]) + _dot(h2_lo, wq_hi_ref[...])
         + _dot(h2_hi, wq_lo_ref[...]))
    for l in range(2 * PEER_HEADS):
        q_hi, q_lo = _split_bf16(q[:, l * PEER_DKH:(l + 1) * PEER_DKH])
        k_hi = keys_hi_ref[l]
        k_lo = keys_lo_ref[l]
        s_scr[l] = _dot_nt(k_hi, q_hi) + _dot_nt(k_hi, q_lo) + _dot_nt(k_lo, q_hi)

    lane_groups = tokens // V7X_LANES

    def stage1(hd, carry):
        for l in (2 * hd, 2 * hd + 1):
            for lg in range(lane_groups):
                lanes = slice(lg * V7X_LANES, (lg + 1) * V7X_LANES)
                vals, idxs = _topk_rows(s_scr[l, :, lanes], PEER_TOPK)
                top_s[l, :, lanes] = vals
                top_i[l, :, lanes] = idxs
        return carry

    lax.fori_loop(0, PEER_HEADS, stage1, 0)

    def stage2(hd, carry):
        rows = pl.ds(pl.multiple_of(hd * PEER_TOPK, PEER_TOPK), PEER_TOPK)
        for lg in range(lane_groups):
            lanes = slice(lg * V7X_LANES, (lg + 1) * V7X_LANES)
            comb, flat = _pair_candidates(top_s[2 * hd, :, lanes], top_s[2 * hd + 1, :, lanes])
            f_s, f_i = _topk_rows(comb, PEER_TOPK, flat)
            ra = jnp.floor(f_i * (1.0 / PEER_TOPK))
            rb = f_i - ra * PEER_TOPK
            eidx = (_lookup_rows(top_i[2 * hd, :, lanes], ra) * PEER_NK
                    + _lookup_rows(top_i[2 * hd + 1, :, lanes], rb))
            p = jnp.exp(f_s - jnp.max(f_s, axis=0, keepdims=True))
            e_scr[rows, lanes] = eidx
            g_scr[rows, lanes] = p / jnp.sum(p, axis=0, keepdims=True)
        return carry

    lax.fori_loop(0, PEER_HEADS, stage2, 0)

    def stage3(lg):
        lanes = slice(lg * V7X_LANES, (lg + 1) * V7X_LANES)
        e = e_scr[:, lanes]
        chunk = jnp.floor(e * (1.0 / CHUNK_EXPERTS))

        def tile_of(ref, kb):
            return ref[pl.ds(pl.multiple_of(kb * V7X_SUBLANES, V7X_SUBLANES), V7X_SUBLANES), lanes]

        field = chunk.astype(jnp.int32) * COUNT_BITS
        ones = jnp.left_shift(jnp.ones_like(field), field)
        slot = lax.broadcasted_iota(jnp.int32, (PEER_SLOTS, V7X_LANES), 0)
        run = ones
        step = 1
        while step < PEER_SLOTS:
            run = run + jnp.where(slot >= step, pltpu.roll(run, step, 0), 0)
            step *= 2
        count_mask = (1 << COUNT_BITS) - 1
        before = (lax.shift_right_logical(run - ones, field) & count_mask).astype(F32)
        totals = run[PEER_SLOTS - 1:PEER_SLOTS, :]
        pstart = jnp.zeros((1, V7X_LANES), F32)
        shift = jnp.zeros_like(e)
        grow = lax.broadcasted_iota(jnp.int32, (V7X_SUBLANES, V7X_LANES), 0)
        gstart = jnp.zeros((V7X_SUBLANES, V7X_LANES), F32)
        for ch in range(PEER_CHUNKS):
            n = (lax.shift_right_logical(totals, ch * COUNT_BITS) & count_mask).astype(F32)
            shift = jnp.where(chunk == float(ch), pstart, shift)
            pstart = pstart + jnp.ceil(n * (1.0 / GROUP)) * GROUP
            gstart = jnp.where(grow == ch + 1, pstart * (1.0 / GROUP), gstart)
        p_scr[:, lanes] = before + shift
        off_scr[:, lanes] = (e - chunk * CHUNK_EXPERTS) * (2 * ROW_TILE)

        list_rows = _row_iota((LIST_ROWS, V7X_LANES))

        def place(kb, lists):
            offs, gates = lists
            p8, off8, gate8 = tile_of(p_scr, kb), tile_of(off_scr, kb), tile_of(g_scr, kb)
            for j in range(V7X_SUBLANES):
                hit = list_rows == p8[j:j + 1, :]
                offs = jnp.where(hit, off8[j:j + 1, :], offs)
                gates = jnp.where(hit, gate8[j:j + 1, :], gates)
            return offs, gates

        zeros = jnp.zeros((LIST_ROWS, V7X_LANES), F32)
        offs, gates = lax.fori_loop(0, PEER_SLOTS // V7X_SUBLANES, place, (zeros, zeros))
        meta_ref[0:LIST_ROWS, lanes] = offs.astype(jnp.int32)
        meta_ref[GSTART_ROW:GSTART_ROW + V7X_SUBLANES, lanes] = gstart.astype(jnp.int32)
        meta_ref[GSTART_ROW + V7X_SUBLANES:, lanes] = jnp.zeros(
            (META_ROWS - GSTART_ROW - V7X_SUBLANES, V7X_LANES), jnp.int32)
        pgate_ref[0:LIST_ROWS, lanes] = gates
        pgate_ref[LIST_ROWS:, lanes] = jnp.zeros((META_ROWS - LIST_ROWS, V7X_LANES), F32)

        lane_i = lax.broadcasted_iota(jnp.int32, (V7X_SUBLANES, V7X_LANES), 1)
        counts = jnp.where(grow < PEER_CHUNKS, pltpu.roll(gstart, V7X_SUBLANES - 1, 0) - gstart, 0.0)
        incl = counts
        step = 1
        while step < V7X_LANES:
            incl = incl + jnp.where(lane_i >= step, pltpu.roll(incl, step, 1), 0.0)
            step *= 2
        delta = gstart - (incl - counts)
        delta_step = pltpu.roll(delta, V7X_LANES - 1, 1) - delta
        columns = jnp.concatenate(
            [incl, delta, delta_step,
             jnp.zeros((V7X_LANES - 3 * V7X_SUBLANES, V7X_LANES), F32)], axis=0).T
        slot_row = lax.broadcasted_iota(jnp.int32, (1, TABLE_SLOTS), 1).astype(F32)
        for ch in range(PEER_CHUNKS):
            ends = columns[:, ch:ch + 1]
            steps_col = columns[:, 2 * V7X_SUBLANES + ch:2 * V7X_SUBLANES + ch + 1]
            passed = ends <= slot_row
            token = jnp.sum(jnp.where(passed, 1.0, 0.0), axis=0, keepdims=True)
            shift_g = (columns[0:1, V7X_SUBLANES + ch:V7X_SUBLANES + ch + 1]
                       + jnp.sum(jnp.where(passed, steps_col, 0.0), axis=0, keepdims=True))
            total = columns[V7X_LANES - 1:V7X_LANES, ch:ch + 1]
            entry = token * META_ROWS + (slot_row + shift_g) * GROUP
            entry = jnp.where(slot_row < total, entry, float(ZERO_ENTRY))
            entry = jnp.where(slot_row == TABLE_SLOTS - 1, total, entry)
            row = lg * PEER_CHUNKS + ch
            tab_ref[row:row + 1, 0:TABLE_SLOTS] = entry.astype(jnp.int32)
            tab_ref[row:row + 1, TABLE_SLOTS:] = jnp.zeros(
                (1, TABLE_STRIDE - TABLE_SLOTS), jnp.int32)

    for lg in range(lane_groups):
        stage3(lg)


def _const_spec(shape):
    zeros = (0,) * len(shape)
    return pl.BlockSpec(shape, lambda i, j: zeros, pipeline_mode=pl.Buffered(1))


def _mixer_router(x, pool_hist, conv_hist, pos0, seqs, steps, weights):
    nseq, length, _ = x.shape
    tokens = seqs * steps
    grid = (nseq // seqs, length // steps)
    blocks_per_seq_group = length // steps
    per_time = pl.BlockSpec((seqs, steps, D_MODEL), lambda i, j: (i, j, 0))
    per_token = pl.BlockSpec((tokens * ROW_TILE, V7X_LANES),
                             lambda i, j: (i * blocks_per_seq_group + j, 0))
    per_seq_pool = pl.BlockSpec((seqs, POOL_HIST, POOL_W), lambda i, j: (i, 0, 0))
    per_seq_conv = pl.BlockSpec((seqs, CONV_HIST, CONV_W), lambda i, j: (i, 0, 0))

    def slot_spec(rows):
        return pl.BlockSpec((rows, tokens), lambda i, j: (0, i * blocks_per_seq_group + j))

    tab_spec = pl.BlockSpec((tokens // PEER_BLOCK * PEER_CHUNKS, TABLE_STRIDE),
                            lambda i, j: (i * blocks_per_seq_group + j, 0))

    n_tok = nseq * length
    kernel = functools.partial(_mixer_router_kernel, pos0=pos0, seqs=seqs, steps=steps)
    slot_scratch = pltpu.VMEM((PEER_SLOTS, tokens), F32)
    return pl.pallas_call(
        kernel,
        grid=grid,
        in_specs=[per_time, per_seq_pool, per_seq_conv] + [_const_spec(w.shape) for w in weights],
        out_specs=[per_time, per_token, slot_spec(META_ROWS), slot_spec(META_ROWS), tab_spec,
                   per_seq_pool, per_seq_conv],
        out_shape=[
            jax.ShapeDtypeStruct((nseq, length, D_MODEL), F32),
            jax.ShapeDtypeStruct((n_tok * ROW_TILE, V7X_LANES), F32),
            jax.ShapeDtypeStruct((META_ROWS, n_tok), jnp.int32),
            jax.ShapeDtypeStruct((META_ROWS, n_tok), F32),
            jax.ShapeDtypeStruct((n_tok // PEER_BLOCK * PEER_CHUNKS, TABLE_STRIDE), jnp.int32),
            jax.ShapeDtypeStruct((nseq, POOL_HIST, POOL_W), F32),
            jax.ShapeDtypeStruct((nseq, CONV_HIST, CONV_W), F32),
        ],
        scratch_shapes=[
            pltpu.VMEM((seqs, POOL_HALO + steps, POOL_W), F32),
            pltpu.VMEM((seqs, CONV_HALO + steps, CONV_W), F32),
            pltpu.VMEM((2 * PEER_HEADS, PEER_NK, tokens), F32),
            pltpu.VMEM((2 * PEER_HEADS, PEER_TOPK, tokens), F32),
            pltpu.VMEM((2 * PEER_HEADS, PEER_TOPK, tokens), F32),
            slot_scratch, slot_scratch, slot_scratch, slot_scratch,
        ],
        compiler_params=pltpu.CompilerParams(
            dimension_semantics=("arbitrary", "arbitrary"),
            vmem_limit_bytes=V7X_VMEM_BYTES * 3 // 4),
        name="mixer_router",
    )(x, pool_hist, conv_hist, *weights)


def _tile_rows(index):
    return pl.ds(pl.multiple_of(index * ROW_TILE, ROW_TILE), ROW_TILE)


def _sum_tiles(p):
    sub = lax.broadcasted_iota(jnp.int32, (V7X_SUBLANES, V7X_LANES), 0)
    m4 = sub < 4
    m2 = (sub & 2) == 0
    m1 = (sub & 1) == 0
    b = [jnp.where(m4, p[i], p[i + 4]) + pltpu.roll(jnp.where(m4, p[i + 4], p[i]), 4, 0)
         for i in range(4)]
    c = [jnp.where(m2, b[i], pltpu.roll(b[i + 2], 2, 0))
         + jnp.where(m2, pltpu.roll(b[i], 6, 0), b[i + 2]) for i in range(2)]
    return (jnp.where(m1, c[0], pltpu.roll(c[1], 1, 0))
            + jnp.where(m1, pltpu.roll(c[0], 7, 0), c[1]))


def _tree_sum(terms):
    while len(terms) > 1:
        terms = [terms[i] + terms[i + 1] for i in range(0, len(terms), 2)]
    return terms[0]


def _peer_experts_kernel(meta_ref, group_tab, gate_ref, h2_ref, acc_ref, uv_ref, out_ref,
                         pair_a, gate_a, v_a, pair_b, gate_b, v_b):
    @pl.when((pl.program_id(0) == 0) & (pl.program_id(1) == 0))
    def _():
        for ref in (pair_a, gate_a, pair_b, gate_b):
            ref[...] = jnp.zeros(ref.shape, F32)

    sub = lax.broadcasted_iota(jnp.int32, (V7X_SUBLANES, V7X_LANES), 0)
    lane = lax.broadcasted_iota(jnp.int32, (V7X_SUBLANES, V7X_LANES), 1)
    lane_minus_sub = [lane - sub - half * V7X_SUBLANES for half in range(GROUP // V7X_SUBLANES)]

    n_groups = group_tab[TABLE_SLOTS - 1]
    n_steps = (n_groups + GROUPS_PER_STEP - 1) // GROUPS_PER_STEP

    def token_of(entry):
        return lax.shift_right_logical(entry, META_ROWS.bit_length() - 1)

    out_ref[...] = acc_ref[...]

    buffers = ((pair_a, gate_a, v_a), (pair_b, gate_b, v_b))

    def dots_step(i, step0, bufs):
        pair_scr, gate_scr, v_scr = bufs
        for q in range(GROUPS_PER_STEP):
            entry = group_tab[i * GROUPS_PER_STEP + q]
            slot = (i - step0) * GROUPS_PER_STEP + q
            h = h2_ref[_tile_rows(token_of(entry)), :]
            gates = gate_ref[pl.ds(lax.shift_right_logical(entry, 7), 1), :]
            first_lane = entry & (V7X_LANES - 1)
            for half in range(GROUP // V7X_SUBLANES):
                prods = []
                for j in range(V7X_SUBLANES):
                    k = half * V7X_SUBLANES + j
                    off = meta_ref[entry + k]
                    uv = uv_ref[pl.ds(pl.multiple_of(off, 2 * ROW_TILE), 2 * ROW_TILE), :]
                    prods.append(uv[:ROW_TILE, :] * h)
                    v_scr[_tile_rows(slot * GROUP + k), :] = uv[ROW_TILE:, :]
                pair_scr[_tile_rows(slot * 2 + half), :] = _sum_tiles(prods)
                own = lane_minus_sub[half] == first_lane
                gate_scr[_tile_rows(slot * 2 + half), :] = jnp.where(own, gates, 0.0)

    def activate(n_batch_steps, bufs):
        pair_scr, gate_scr, _ = bufs

        def slab(i, carry):
            rows = pl.ds(pl.multiple_of(i * SLAB_ROWS, SLAB_ROWS), SLAB_ROWS)
            act = jnp.sum(pair_scr[rows, :], axis=-1, keepdims=True)
            gate = jnp.sum(gate_scr[rows, :], axis=-1, keepdims=True)
            pair_scr[rows, :] = jnp.broadcast_to(gate * _gelu_tanh(act), (SLAB_ROWS, V7X_LANES))
            return carry

        n_rows = n_batch_steps * (GROUPS_PER_STEP * GROUP)
        lax.fori_loop(0, (n_rows + SLAB_ROWS - 1) // SLAB_ROWS, slab, 0)

    def mix_step(i, step0, bufs):
        pair_scr, _, v_scr = bufs
        for q in range(GROUPS_PER_STEP):
            entry = group_tab[i * GROUPS_PER_STEP + q]
            slot = (i - step0) * GROUPS_PER_STEP + q
            terms = []
            for half in range(GROUP // V7X_SUBLANES):
                coef = pair_scr[_tile_rows(slot * 2 + half), :]
                for j in range(V7X_SUBLANES):
                    k = half * V7X_SUBLANES + j
                    cj = jnp.broadcast_to(coef[j:j + 1, :], (V7X_SUBLANES, V7X_LANES))
                    terms.append(cj * v_scr[_tile_rows(slot * GROUP + k), :])
            rows = _tile_rows(token_of(entry))
            out_ref[rows, :] = out_ref[rows, :] + _tree_sum(terms)

    def loop_steps(step0, step1, bufs, step_fn):
        def body(i, carry):
            step_fn(i, step0, bufs)
            return carry
        lax.fori_loop(step0, step1, body, 0)

    n_full = n_steps // BATCH_STEPS

    def overlapped(b, mine, previous):
        activate(BATCH_STEPS, previous)

        def body(i, carry):
            dots_step(b * BATCH_STEPS + i, b * BATCH_STEPS, mine)
            mix_step((b - 1) * BATCH_STEPS + i, (b - 1) * BATCH_STEPS, previous)
            return carry

        lax.fori_loop(0, BATCH_STEPS, body, 0)

    @pl.when(n_full > 0)
    def _():
        loop_steps(0, BATCH_STEPS, buffers[0], dots_step)

    def batch_pair(bb, carry):
        b = 2 * bb + 1
        overlapped(b, buffers[1], buffers[0])

        @pl.when(b + 1 < n_full)
        def _():
            overlapped(b + 1, buffers[0], buffers[1])

        return carry

    lax.fori_loop(0, n_full // 2, batch_pair, 0)

    for parity in range(2):
        @pl.when((n_full > 0) & ((n_full - 1) % 2 == parity))
        def _(parity=parity):
            last0 = (n_full - 1) * BATCH_STEPS
            activate(BATCH_STEPS, buffers[parity])
            loop_steps(last0, last0 + BATCH_STEPS, buffers[parity], mix_step)

    tail0 = n_full * BATCH_STEPS

    @pl.when(tail0 < n_steps)
    def _():
        loop_steps(tail0, n_steps, buffers[0], dots_step)
        activate(n_steps - tail0, buffers[0])
        loop_steps(tail0, n_steps, buffers[0], mix_step)


def _peer_experts(meta, tables, pgate, h2_tiles, peer_uv):
    n_tok = meta.shape[0] // META_ROWS
    tok_spec = pl.BlockSpec((PEER_BLOCK * ROW_TILE, V7X_LANES), lambda c, i: (i, 0))
    table_spec = pl.BlockSpec((CHUNK_EXPERTS * 2 * ROW_TILE, V7X_LANES), lambda c, i: (c, 0),
                              pipeline_mode=pl.Buffered(1))
    batch_groups = BATCH_STEPS * GROUPS_PER_STEP
    pair_rows = batch_groups * GROUP
    assert pair_rows % SLAB_ROWS == 0
    v_rows = pair_rows * ROW_TILE
    table_bytes = 2 * CHUNK_EXPERTS * D_MODEL * 4
    block_bytes = PEER_BLOCK * D_MODEL * 4
    buffer_set = [pltpu.VMEM((pair_rows, V7X_LANES), F32),
                  pltpu.VMEM((pair_rows, V7X_LANES), F32),
                  pltpu.VMEM((v_rows, V7X_LANES), F32)]
    vmem_bytes = (table_bytes + 2 * (2 * pair_rows + v_rows) * V7X_LANES * 4 + 7 * block_bytes
                  + (4 << 20))
    assert vmem_bytes <= V7X_VMEM_BYTES
    gate_spec = pl.BlockSpec((PEER_BLOCK * META_ROWS // V7X_LANES, V7X_LANES), lambda c, i: (i, 0))
    return pl.pallas_call(
        _peer_experts_kernel,
        grid=(PEER_CHUNKS, n_tok // PEER_BLOCK),
        in_specs=[
            pl.BlockSpec((PEER_BLOCK * META_ROWS,), lambda c, i: (i,), memory_space=pltpu.SMEM),
            pl.BlockSpec((TABLE_STRIDE,), lambda c, i: (i * PEER_CHUNKS + c,),
                         memory_space=pltpu.SMEM),
            gate_spec, tok_spec, tok_spec, table_spec,
        ],
        out_specs=tok_spec,
        out_shape=jax.ShapeDtypeStruct((n_tok * ROW_TILE, V7X_LANES), F32),
        scratch_shapes=buffer_set + buffer_set,
        input_output_aliases={4: 0},
        compiler_params=pltpu.CompilerParams(
            dimension_semantics=("arbitrary", "arbitrary"),
            vmem_limit_bytes=vmem_bytes),
        name="peer_experts",
    )(meta, tables, pgate, h2_tiles, jnp.zeros((n_tok * ROW_TILE, V7X_LANES), F32), peer_uv)


def _final_norm_kernel(x1_ref, peer_ref, g_ref, y_ref):
    peer = jnp.concatenate(
        [peer_ref[pl.ds(j, NORM_BLOCK, stride=ROW_TILE), :] for j in range(ROW_TILE)], axis=-1)
    y_ref[...] = _rms(x1_ref[...] + peer, g_ref[...])


def _final_norm(x1, peer_tiles, g_final):
    n_tok = x1.shape[0]
    spec = pl.BlockSpec((NORM_BLOCK, D_MODEL), lambda i: (i, 0))
    tile_spec = pl.BlockSpec((NORM_BLOCK * ROW_TILE, V7X_LANES), lambda i: (i, 0))
    return pl.pallas_call(
        _final_norm_kernel,
        grid=(n_tok // NORM_BLOCK,),
        in_specs=[spec, tile_spec, pl.BlockSpec((1, D_MODEL), lambda i: (0, 0))],
        out_specs=spec,
        out_shape=jax.ShapeDtypeStruct((n_tok, D_MODEL), F32),
        name="final_norm",
    )(x1, peer_tiles, g_final)


def _trunk(x, pool_hist, conv_hist, pos0, seqs, steps, weights, g_final, uv_tiles):
    nseq, length, _ = x.shape
    n_tok = nseq * length
    x1, h2_tiles, meta_t, pgate_t, tables, new_pool, new_conv = _mixer_router(
        x, pool_hist, conv_hist, pos0, seqs, steps, weights)
    peer_tiles = _peer_experts(meta_t.T.reshape(-1), tables.reshape(-1),
                               pgate_t.T.reshape(-1, V7X_LANES), h2_tiles, uv_tiles)
    y = _final_norm(x1.reshape(n_tok, D_MODEL), peer_tiles, g_final)
    return y.reshape(nseq, length, D_MODEL), new_pool[None], new_conv[None]


def kernel(x_prompt, x_sample, state_pool, state_conv, g_mix, w_in, b_in, w_pool, b_pool,
           pool_scale, w_dw, b_dw, ln_g, ln_b, w_out, g_ffn, w_query, sub_keys, peer_u,
           peer_v, g_final):
    assert g_mix.shape[0] == 1, "single-layer trunk"
    wq_hi, wq_lo = _split_bf16(w_query[0])
    keys = sub_keys[0].reshape(2 * PEER_HEADS, PEER_NK, PEER_DKH)
    keys_hi, keys_lo = _split_bf16(keys)
    row = lambda v: v.reshape(1, -1)
    weights = (
        row(g_mix[0]), w_in[0].astype(BF16), row(b_in[0]), w_pool[0].astype(BF16),
        row(b_pool[0]), row(pool_scale[0]), w_dw[0], row(b_dw[0]), row(ln_g[0]),
        row(ln_b[0]), w_out[0].astype(BF16), row(g_ffn[0]), wq_hi, wq_lo, keys_hi, keys_lo)
    g_final_row = row(g_final)
    uv_tiles = jnp.stack(
        [peer_u[0].reshape(PEER_N, ROW_TILE, V7X_LANES),
         peer_v[0].reshape(PEER_N, ROW_TILE, V7X_LANES)], axis=1,
    ).reshape(PEER_N * 2 * ROW_TILE, V7X_LANES)

    n_prompt = x_prompt.shape[0]
    zero_pool = jnp.zeros((n_prompt, POOL_HIST, POOL_W), F32)
    zero_conv = jnp.zeros((n_prompt, CONV_HIST, CONV_W), F32)
    y_p, pool_p, conv_p = _trunk(
        x_prompt, zero_pool, zero_conv, 0, 1, BLOCK_TOKENS, weights, g_final_row, uv_tiles)
    dec_seq = x_sample.shape[1]
    y_s, pool_s, conv_s = _trunk(
        x_sample, state_pool[0], state_conv[0], PAST_LEN, BLOCK_TOKENS // dec_seq, dec_seq,
        weights, g_final_row, uv_tiles)
    return (y_p, y_s, pool_p, conv_p, pool_s, conv_s)
```
